```python
import math
import jax, jax.numpy as jnp
from jax import lax
import numpy as np

D_MODEL = 1024
BATCH = 4
SEQ = 4096
DEPTH = 2

POOL_WIDTH = 256
POOL_GROUPS = 4
POOL_WINDOWS = (2, 4, 8, 16)
POOL_GROUP_DIM = POOL_WIDTH // POOL_GROUPS
HGRN_HEADS = 4
HGRN_DK = 64
HGRN_DV = 64
HGRN_WIDTH = HGRN_HEADS * HGRN_DV
HGRN_CHUNK = 64
MIN_F = 1e-20
CONV_WIDTH = 256
CONV_K = 3
DIFF_HEADS = 4
DIFF_DH = 64
DIFF_WIDTH = DIFF_HEADS * 2 * DIFF_DH
Q_BLOCK = 128

N_BRANCH = 4
EPS = 1e-6
NEG = -1e30

IN_SPLITS = (
    POOL_WIDTH, POOL_WIDTH,
    HGRN_HEADS * HGRN_DK, HGRN_HEADS * HGRN_DK, HGRN_WIDTH, HGRN_WIDTH,
    CONV_WIDTH, CONV_WIDTH, CONV_WIDTH, CONV_WIDTH,
    DIFF_WIDTH, DIFF_WIDTH, DIFF_WIDTH, DIFF_WIDTH,
    N_BRANCH * D_MODEL,
)
D_IN = sum(IN_SPLITS)
IN_OFFSETS = tuple(int(v) for v in np.cumsum(IN_SPLITS)[:-1])

kernel_name = "hybrid_pool_hgrn2_conv_diffattn_block"


def rmsnorm(x, g):
    xf = x.astype(jnp.float32)
    y = xf * lax.rsqrt(jnp.mean(xf * xf, axis=-1, keepdims=True) + EPS)
    return (y * g.astype(jnp.float32)).astype(x.dtype)


def pool_mixer(a, pool_w, pool_scale):
    B_, T, _ = a.shape
    af = a.astype(jnp.float32).reshape(B_, T, POOL_GROUPS, POOL_GROUP_DIM)
    cs = jnp.concatenate([jnp.zeros_like(af[:, :1]), jnp.cumsum(af, axis=1)], axis=1)
    pos = jnp.arange(T)
    outs = []
    for gi, w in enumerate(POOL_WINDOWS):
        hi = cs[:, 1:, gi]
        lo = jnp.concatenate([jnp.zeros((B_, w - 1, POOL_GROUP_DIM), jnp.float32),
                              cs[:, :T - w + 1, gi]], axis=1)
        count = jnp.minimum(pos + 1, w).astype(jnp.float32)[None, :, None]
        outs.append((hi - lo) / count - af[:, :, gi])
    pooled = jnp.stack(outs, axis=2)
    mixed = jnp.einsum('btgc,gcd->btgd', pooled, pool_w.astype(jnp.float32))
    y = mixed.reshape(B_, T, POOL_WIDTH) * pool_scale.astype(jnp.float32)
    return y.astype(a.dtype)


def hgrn2_mixer(q, zf, v, lb, norm_g):
    B_, T, _ = q.shape
    f32 = jnp.float32
    nc = T // HGRN_CHUNK
    lbf = lb.astype(f32)
    z = zf.astype(f32)
    f = lbf + (1.0 - lbf) * jax.nn.sigmoid(z)
    log_f = jnp.log(jnp.maximum(f, MIN_F))
    k = (1.0 - lbf) * jax.nn.sigmoid(-z)
    qf = jax.nn.silu(q.astype(f32))

    def to_chunks(t, d):
        return t.reshape(B_, nc, HGRN_CHUNK, HGRN_HEADS, d).transpose(1, 0, 3, 2, 4)

    xs = (to_chunks(qf, HGRN_DK), to_chunks(k, HGRN_DK),
          to_chunks(v.astype(f32), HGRN_DV), to_chunks(log_f, HGRN_DK))
    causal = jnp.tril(jnp.ones((HGRN_CHUNK, HGRN_CHUNK), bool))[:, :, None]

    def step(S, inp):
        qc, kc, vc, gc = inp
        b = jnp.cumsum(gc, axis=2)
        o_inter = jnp.einsum('bhtk,bhkv->bhtv', qc * jnp.exp(b), S)
        diff = b[:, :, :, None, :] - b[:, :, None, :, :]
        decay = jnp.exp(jnp.where(causal, diff, NEG))
        scores = jnp.einsum('bhtk,bhtsk,bhsk->bhts', qc, decay, kc)
        o = o_inter + jnp.einsum('bhts,bhsv->bhtv', scores, vc)
        b_last = b[:, :, -1:, :]
        S_new = (jnp.exp(b_last[:, :, 0, :])[..., None] * S
                 + jnp.einsum('bhsk,bhsv->bhkv', kc * jnp.exp(b_last - b), vc))
        return S_new, o

    S0 = jnp.zeros((B_, HGRN_HEADS, HGRN_DK, HGRN_DV), f32)
    _, o = lax.scan(step, S0, xs)
    o = o.transpose(1, 0, 3, 2, 4).reshape(B_, T, HGRN_HEADS, HGRN_DV)
    o = rmsnorm(o, norm_g)
    return o.reshape(B_, T, HGRN_WIDTH).astype(q.dtype)


def short_conv_mixer(u, gate_b, gate_c, conv_w):
    z = gate_c * u
    y = lax.conv_general_dilated(z, conv_w[:, None, :], window_strides=(1,),
                                 padding=[(CONV_K - 1, 0)],
                                 dimension_numbers=('NWC', 'WIO', 'NWC'),
                                 feature_group_count=CONV_WIDTH)
    return gate_b * y


def diff_attention(q, k, v, lam, norm_g, layer_idx):
    B_, T, _ = q.shape
    f32 = jnp.float32
    q = q.reshape(B_, T, DIFF_HEADS, 2, DIFF_DH)
    k = k.reshape(B_, T, DIFF_HEADS, 2, DIFF_DH)
    v = v.reshape(B_, T, DIFF_HEADS, 2 * DIFF_DH)
    lam_init = 0.8 - 0.6 * math.exp(-0.3 * layer_idx)
    lf = lam.astype(f32)
    lam_full = jnp.exp(jnp.sum(lf[0] * lf[1])) - jnp.exp(jnp.sum(lf[2] * lf[3])) + lam_init
    scale = DIFF_DH ** -0.5
    qpos = jnp.arange(Q_BLOCK)
    outs = []
    for i in range(T // Q_BLOCK):
        L = (i + 1) * Q_BLOCK
        qb = q[:, i * Q_BLOCK:L]
        s = jnp.einsum('bqhmd,bkhmd->bhmqk', qb, k[:, :L]).astype(f32) * scale
        mask = jnp.arange(L)[None, :] <= (i * Q_BLOCK + qpos)[:, None]
        p = jax.nn.softmax(jnp.where(mask, s, NEG), axis=-1)
        a = p[:, :, 0] - lam_full * p[:, :, 1]
        outs.append(jnp.einsum('bhqk,bkhe->bqhe', a.astype(v.dtype), v[:, :L]))
    o = jnp.concatenate(outs, axis=1)
    o = rmsnorm(o, norm_g) * (1.0 - lam_init)
    return o.reshape(B_, T, DIFF_WIDTH)


def setup_inputs(seed: int = 0) -> dict:
    key = jax.random.key(seed)
    ks = jax.random.split(key, 20)

    def nrm(k, shape, s):
        return jax.random.normal(k, shape, jnp.float32) * s

    D = D_MODEL
    return {
        "x": nrm(ks[0], (BATCH, SEQ, D), 1.0),
        "c": nrm(ks[1], (BATCH, D), 1.0),
        "w_ada": nrm(ks[2], (DEPTH, D, 3 * D), 0.5 * D ** -0.5),
        "b_ada": nrm(ks[3], (DEPTH, 3 * D), 0.02),
        "g_pre": 1.0 + nrm(ks[4], (DEPTH, D), 0.1),
        "g_post": 1.0 + nrm(ks[5], (DEPTH, D), 0.1),
        "w_in": nrm(ks[6], (DEPTH, D, D_IN), D ** -0.5),
        "pool_w": nrm(ks[7], (DEPTH, POOL_GROUPS, POOL_GROUP_DIM, POOL_GROUP_DIM), POOL_GROUP_DIM ** -0.5),
        "pool_scale": 1.0 + nrm(ks[8], (DEPTH, POOL_WIDTH), 0.1),
        "hgrn_lb": nrm(ks[9], (DEPTH, HGRN_HEADS * HGRN_DK), 1.0),
        "hgrn_norm": 1.0 + nrm(ks[10], (DEPTH, HGRN_DV), 0.1),
        "conv_w": nrm(ks[11], (DEPTH, CONV_K, CONV_WIDTH), CONV_K ** -0.5),
        "diff_lam": nrm(ks[12], (DEPTH, 4, DIFF_DH), 0.1),
        "diff_norm": 1.0 + nrm(ks[13], (DEPTH, 2 * DIFF_DH), 0.1),
        "w_merge_pool": nrm(ks[14], (DEPTH, POOL_WIDTH, D), POOL_WIDTH ** -0.5),
        "w_merge_hgrn": nrm(ks[15], (DEPTH, HGRN_WIDTH, D), HGRN_WIDTH ** -0.5),
        "w_merge_conv": nrm(ks[16], (DEPTH, CONV_WIDTH, D), CONV_WIDTH ** -0.5),
        "w_merge_diff": nrm(ks[17], (DEPTH, DIFF_WIDTH, D), DIFF_WIDTH ** -0.5),
        "w_out": nrm(ks[18], (DEPTH, D, D), D ** -0.5),
    }


def reference(x, c, w_ada, b_ada, g_pre, g_post, w_in, pool_w, pool_scale, hgrn_lb,
              hgrn_norm, conv_w, diff_lam, diff_norm, w_merge_pool, w_merge_hgrn,
              w_merge_conv, w_merge_diff, w_out):
    B_, T, D = x.shape
    c_act = jax.nn.silu(c)
    lb_sm = jax.nn.softmax(hgrn_lb.astype(jnp.float32), axis=0)
    lower_bounds = jnp.cumsum(lb_sm, axis=0) - lb_sm[0]
    for l in range(DEPTH):
        mod = c_act @ w_ada[l] + b_ada[l]
        shift, scale, gate = jnp.split(mod, 3, axis=-1)
        h = rmsnorm(x, g_pre[l]) * (1.0 + scale[:, None]) + shift[:, None]
        p = h @ w_in[l]
        (a_in, a_g, b_q, b_f, b_i, b_g, c_x, c_b, c_c, c_g,
         d_q, d_k, d_v, d_g, m_g) = jnp.split(p, IN_OFFSETS, axis=-1)

        y_pool = pool_mixer(a_in, pool_w[l], pool_scale[l]) * jax.nn.silu(a_g)
        y_hgrn = hgrn2_mixer(b_q, b_f, b_i, lower_bounds[l], hgrn_norm[l]) * jax.nn.silu(b_g)
        y_conv = short_conv_mixer(c_x, c_b, c_c, conv_w[l]) * jax.nn.silu(c_g)
        y_diff = diff_attention(d_q, d_k, d_v, diff_lam[l], diff_norm[l], l) * jax.nn.silu(d_g)

        gates = jax.nn.sigmoid(m_g.reshape(B_, T, N_BRANCH, D))
        merged = (gates[:, :, 0] * (y_pool @ w_merge_pool[l])
                  + gates[:, :, 1] * (y_hgrn @ w_merge_hgrn[l])
                  + gates[:, :, 2] * (y_conv @ w_merge_conv[l])
                  + gates[:, :, 3] * (y_diff @ w_merge_diff[l]))
        out = merged @ w_out[l]
        x = x + gate[:, None] * rmsnorm(out, g_post[l])
    return x
```

```python
import functools
import math

import numpy as np
import jax
import jax.numpy as jnp
from jax import lax
from jax.experimental import pallas as pl
from jax.experimental.pallas import tpu as pltpu

F32 = jnp.float32
BF16 = jnp.bfloat16

POOL_WIDTH = 256
POOL_WINDOWS = (2, 4, 8, 16)
POOL_GROUP_DIM = 64
HGRN_HEADS = 4
HGRN_DK = 64
HGRN_WIDTH = 256
MIN_F = 1e-20
CONV_WIDTH = 256
CONV_K = 3
DIFF_HEADS = 4
DIFF_DH = 64
DIFF_WIDTH = 512
N_BRANCH = 4
EPS = 1e-6
NEG = -1e30
LOG2E = 1.4426950408889634

OFF_POOL_IN, OFF_POOL_G = 0, 256
OFF_HQ, OFF_HF, OFF_HI, OFF_HG = 512, 768, 1024, 1280
OFF_CX, OFF_CB, OFF_CC, OFF_CG = 1536, 1792, 2048, 2304
OFF_DQ, OFF_DK, OFF_DV, OFF_DG = 2560, 3072, 3584, 4096
MIX_COLS = 4608

LANES = 128
SUBLANES = 8
HALO = 16
HGRN_CHUNK = 64
VMEM_LIMIT = 56 * 1024 * 1024


def _sigmoid(x):
    return 1.0 / (1.0 + jnp.exp(-x))


def _silu(x):
    return x * _sigmoid(x)


def _rms(x, g):
    return x * lax.rsqrt(jnp.mean(x * x, axis=-1, keepdims=True) + EPS) * g


def _dot(a, b):
    return jnp.dot(a, b, preferred_element_type=F32)


def _dot_nt(a, b):
    return lax.dot_general(a, b, (((1,), (1,)), ((), ())), preferred_element_type=F32)


def _split3(x):
    hi = x.astype(BF16)
    r1 = x - hi.astype(F32)
    mid = r1.astype(BF16)
    lo = (r1 - mid.astype(F32)).astype(BF16)
    return hi, mid, lo


def _mod_kernel(c_ref, w_ref, b_ref, o_ref):
    ca = _silu(c_ref[...])
    o_ref[0] = jnp.dot(ca, w_ref[0], preferred_element_type=F32,
                       precision=lax.Precision.HIGHEST) + b_ref[0]


def _modulation(c, w_ada, b_ada):
    depth, d, d3 = w_ada.shape
    bsz = c.shape[0]
    tn = 1024
    return pl.pallas_call(
        _mod_kernel,
        out_shape=jax.ShapeDtypeStruct((depth, bsz, d3), F32),
        grid=(depth, d3 // tn),
        in_specs=[
            pl.BlockSpec((bsz, d), lambda l, j: (0, 0)),
            pl.BlockSpec((1, d, tn), lambda l, j: (l, 0, j)),
            pl.BlockSpec((1, 1, tn), lambda l, j: (l, 0, j)),
        ],
        out_specs=pl.BlockSpec((1, bsz, tn), lambda l, j: (l, 0, j)),
        compiler_params=pltpu.CompilerParams(
            dimension_semantics=("parallel", "parallel"), vmem_limit_bytes=VMEM_LIMIT),
        name="adaln_mod",
    )(c, w_ada, b_ada.reshape(depth, 1, d3))


def _prenorm(x, mod_row, g_pre, d):
    shift = mod_row[:, 0:d]
    scale = mod_row[:, d:2 * d]
    return _rms(x, g_pre) * (1.0 + scale) + shift


def _inproj_kernel(x_ref, mod_ref, g_ref, w_ref, o_ref, h_ref, *, d):
    @pl.when(pl.program_id(1) == 0)
    def _():
        h_ref[...] = _prenorm(x_ref[...], mod_ref[0], g_ref[...], d).astype(BF16)

    o_ref[...] = _dot(h_ref[...], w_ref[...])


def _inproj(x2, mod_l, g_pre, w_mix, *, seq, tm, tn):
    m, d = x2.shape
    n = w_mix.shape[1]
    tiles_per_batch = seq // tm
    return pl.pallas_call(
        functools.partial(_inproj_kernel, d=d),
        out_shape=jax.ShapeDtypeStruct((m, n), F32),
        grid=(m // tm, n // tn),
        in_specs=[
            pl.BlockSpec((tm, d), lambda i, j: (i, 0)),
            pl.BlockSpec((1, 1, 3 * d), lambda i, j: (i // tiles_per_batch, 0, 0)),
            pl.BlockSpec((1, d), lambda i, j: (0, 0)),
            pl.BlockSpec((d, tn), lambda i, j: (0, j)),
        ],
        out_specs=pl.BlockSpec((tm, tn), lambda i, j: (i, j)),
        scratch_shapes=[pltpu.VMEM((tm, d), BF16)],
        compiler_params=pltpu.CompilerParams(
            dimension_semantics=("parallel", "arbitrary"), vmem_limit_bytes=VMEM_LIMIT),
        name="in_proj",
    )(x2, mod_l, g_pre, w_mix)


def _attn_kernel(q_ref, k_ref, v_ref, g_ref, lam_ref, ng_ref, o_ref, *, tq, lam_init):
    i = pl.program_id(2)
    lane = lax.broadcasted_iota(jnp.int32, (1, LANES), 1)
    qs = q_ref[...] * (DIFF_DH ** -0.5 * LOG2E)
    q_maps = (jnp.where(lane < DIFF_DH, qs, 0.0).astype(BF16),
              jnp.where(lane >= DIFF_DH, qs, 0.0).astype(BF16))

    def step(j, carry, masked):
        kj = k_ref[pl.ds(pl.multiple_of(j * tq, tq), tq), :].astype(BF16)
        vj = v_ref[pl.ds(pl.multiple_of(j * tq, tq), tq), :].astype(BF16)
        if masked:
            row = lax.broadcasted_iota(jnp.int32, (tq, tq), 0)
            col = lax.broadcasted_iota(jnp.int32, (tq, tq), 1)
            keep = col <= row
        out = []
        for mp in range(2):
            m_old, l_old, acc = carry[mp]
            s = _dot_nt(q_maps[mp], kj)
            if masked:
                s = jnp.where(keep, s, NEG)
            m_new = jnp.maximum(m_old, jnp.max(s, axis=-1, keepdims=True))
            alpha = jnp.exp2(m_old - m_new)
            p = jnp.exp2(s - m_new)
            l_new = alpha * l_old + jnp.sum(p, axis=-1, keepdims=True)
            acc = alpha * acc + _dot(p.astype(BF16), vj)
            out.append((m_new, l_new, acc))
        return tuple(out)

    init = tuple((jnp.full((tq, 1), NEG, F32), jnp.zeros((tq, 1), F32),
                  jnp.zeros((tq, LANES), F32)) for _ in range(2))
    carry = lax.fori_loop(0, i, lambda j, c: step(j, c, False), init)
    (_, l0, a0), (_, l1, a1) = step(i, carry, True)

    lf = lam_ref[...]
    lam_full = (jnp.exp(jnp.sum(lf[0:1] * lf[1:2], axis=-1, keepdims=True))
                - jnp.exp(jnp.sum(lf[2:3] * lf[3:4], axis=-1, keepdims=True)) + lam_init)
    o = a0 / l0 - lam_full * (a1 / l1)
    o = _rms(o, ng_ref[...]) * (1.0 - lam_init)
    o_ref[...] = o * _silu(g_ref[...])


def _diff_attention(p, lam, norm_g, layer_idx, *, bsz, seq, tq):
    m = p.shape[0]
    nq = seq // tq
    lam_init = 0.8 - 0.6 * math.exp(-0.3 * layer_idx)
    cq, ck, cv, cg = (OFF_DQ // LANES, OFF_DK // LANES, OFF_DV // LANES, OFF_DG // LANES)
    return pl.pallas_call(
        functools.partial(_attn_kernel, tq=tq, lam_init=lam_init),
        out_shape=jax.ShapeDtypeStruct((m, DIFF_WIDTH), F32),
        grid=(bsz, DIFF_HEADS, nq),
        in_specs=[
            pl.BlockSpec((tq, LANES), lambda b, h, i: (b * nq + i, cq + h)),
            pl.BlockSpec((seq, LANES), lambda b, h, i: (b, ck + h)),
            pl.BlockSpec((seq, LANES), lambda b, h, i: (b, cv + h)),
            pl.BlockSpec((tq, LANES), lambda b, h, i: (b * nq + i, cg + h)),
            pl.BlockSpec((4, DIFF_DH), lambda b, h, i: (0, 0)),
            pl.BlockSpec((1, LANES), lambda b, h, i: (0, 0)),
        ],
        out_specs=pl.BlockSpec((tq, LANES), lambda b, h, i: (b * nq + i, h)),
        compiler_params=pltpu.CompilerParams(
            dimension_semantics=("parallel", "parallel", "arbitrary"),
            vmem_limit_bytes=VMEM_LIMIT),
        name="diff_attn",
    )(p, p, p, p, lam, norm_g.reshape(1, LANES))


def _hgrn_constants():
    c = HGRN_CHUNK
    t = np.arange(c)
    mats = [t[None, :] <= t[:, None], t[None, :] > t[:, None]]
    q_mats, k_mats, masks = [], [], [np.eye(c, dtype=bool)]
    m = 1
    while m < c:
        blk = t // m
        odd = blk % 2 == 1
        q_mats.append(odd[:, None] & (t[None, :] >= (m * blk)[:, None]) & (t[None, :] <= t[:, None]))
        k_mats.append((~odd)[:, None] & (t[None, :] > t[:, None])
                      & (t[None, :] <= (m * (blk + 1) - 1)[:, None]))
        masks.append(odd[:, None] & (blk[None, :] == (blk - 1)[:, None]))
        m *= 2
    wsum = np.concatenate(mats + q_mats + k_mats, axis=0).astype(np.float32)
    masks = np.stack([np.tile(mk, (1, HGRN_HEADS)) for mk in masks]).astype(np.float32)
    head = np.arange(HGRN_WIDTH) // HGRN_DK
    bmask = (head[:, None] == head[None, :]).astype(np.float32)
    return wsum, masks, bmask


def _hgrn_kernel(q_ref, f_ref, i_ref, g_ref, lb_ref, ng_ref, wsum_ref, masks_ref, bmask_ref,
                 o_ref, st_ref, *, layer_idx, n_chunks):
    c = HGRN_CHUNK
    n_levels = masks_ref.shape[0] - 1

    @pl.when(pl.program_id(1) == 0)
    def _():
        st_ref[...] = jnp.zeros_like(st_ref)

    lb_all = lb_ref[...]
    e = jnp.exp(lb_all - jnp.max(lb_all, axis=0, keepdims=True))
    sm = e / jnp.sum(e, axis=0, keepdims=True)
    lb = sm[0:1]
    for r in range(1, layer_idx + 1):
        lb = lb + sm[r:r + 1]
    lb = lb - sm[0:1]

    bmask = bmask_ref[...]
    wsum = wsum_ref[...]
    ng = ng_ref[...]

    def chunk(ci, _):
        rows = pl.ds(pl.multiple_of(ci * c, c), c)
        z = f_ref[rows, :]
        v = i_ref[rows, :]
        q = q_ref[rows, :]
        f = lb + (1.0 - lb) * _sigmoid(z)
        g = jnp.log(jnp.maximum(f, MIN_F))
        kk = (1.0 - lb) * _sigmoid(-z)
        qf = _silu(q)

        d = sum(_dot(wsum, part) for part in _split3(g))
        b = d[0:c]
        b_rest = d[c:2 * c]
        b_last = b[c - 1:c]

        vb = v.astype(BF16)
        v_bd = (jnp.concatenate([v] * HGRN_HEADS, axis=0) * bmask).astype(BF16)

        a_cat = jnp.zeros((c, HGRN_WIDTH), F32)
        for lv in range(n_levels + 1):
            if lv == 0:
                qs, ks = qf, kk
            else:
                dq = d[(1 + lv) * c:(2 + lv) * c]
                dk = d[(1 + n_levels + lv) * c:(2 + n_levels + lv) * c]
                qs, ks = qf * jnp.exp(dq), kk * jnp.exp(dk)
            ks_bd = (jnp.concatenate([ks] * HGRN_HEADS, axis=0) * bmask).astype(BF16)
            a_cat = a_cat + masks_ref[lv] * _dot_nt(qs.astype(BF16), ks_bd)

        st = st_ref[...]
        o = _dot_nt((qf * jnp.exp(b)).astype(BF16), st.astype(BF16)) + _dot(a_cat.astype(BF16), v_bd)
        kd = (kk * jnp.exp(b_rest)).astype(BF16)
        st_ref[...] = st * jnp.exp(b_last) + bmask * _dot(vb.T, kd)

        ms = _dot(o * o, bmask * (1.0 / HGRN_DK))
        y = o * lax.rsqrt(ms + EPS) * ng
        o_ref[rows, :] = y * _silu(g_ref[rows, :])
        return 0

    lax.fori_loop(0, n_chunks, chunk, 0)


def _hgrn(p, hgrn_lb, norm_g, layer_idx, *, bsz, seq, tt):
    m = p.shape[0]
    nt = seq // tt
    wsum, masks, bmask = _hgrn_constants()
    w = HGRN_WIDTH
    col = lambda off: off // w
    spec = lambda off: pl.BlockSpec((tt, w), lambda b, t, o=off: (b * nt + t, col(o)))
    whole = lambda a: pl.BlockSpec(a.shape, lambda b, t, nd=a.ndim: (0,) * nd)
    ng = jnp.tile(norm_g.reshape(1, HGRN_DK), (1, HGRN_HEADS))
    consts = (hgrn_lb, ng, jnp.asarray(wsum, BF16), jnp.asarray(masks), jnp.asarray(bmask))
    return pl.pallas_call(
        functools.partial(_hgrn_kernel, layer_idx=layer_idx, n_chunks=tt // HGRN_CHUNK),
        out_shape=jax.ShapeDtypeStruct((m, w), F32),
        grid=(bsz, nt),
        in_specs=[spec(OFF_HQ), spec(OFF_HF), spec(OFF_HI), spec(OFF_HG)] + [whole(a) for a in consts],
        out_specs=pl.BlockSpec((tt, w), lambda b, t: (b * nt + t, 0)),
        scratch_shapes=[pltpu.VMEM((w, w), F32)],
        compiler_params=pltpu.CompilerParams(
            dimension_semantics=("parallel", "arbitrary"), vmem_limit_bytes=VMEM_LIMIT),
        name="hgrn2",
    )(p, p, p, p, *consts)


def _history(cur, halo, first):
    return jnp.concatenate([jnp.where(first, 0.0, halo), cur], axis=0)


def _shift_rows(x, k):
    return pltpu.roll(x, k, 0)


def _tail_kernel(x_ref, mod_ref, gpre_ref, gpost_ref,
                 ain_ref, ag_ref, ainh_ref, cx_ref, cb_ref, cc_ref, cg_ref, cxh_ref, cch_ref,
                 yh_ref, yd_ref,
                 wmg_ref, poolw_ref, pools_ref, convw_ref,
                 wmp_ref, wmh_ref, wmc_ref, wmd_ref, wout_ref,
                 o_ref, *, d, tm, tiles_per_batch):
    t_idx = pl.program_id(0) % tiles_per_batch
    first = t_idx == 0
    x = x_ref[...]
    mod = mod_ref[0]
    h = _prenorm(x, mod, gpre_ref[...], d).astype(BF16)

    a = ain_ref[...]
    s = _history(a, ainh_ref[...], first)
    lane = lax.broadcasted_iota(jnp.int32, (1, POOL_WIDTH), 1)
    grp = lane // POOL_GROUP_DIM
    win = None
    for gi, w in enumerate(POOL_WINDOWS):
        s = s + _shift_rows(s, w // 2)
        win = s if win is None else jnp.where(grp >= gi, s, win)
    win = win[HALO:]
    wlen = jnp.where(grp == 0, POOL_WINDOWS[0], jnp.where(grp == 1, POOL_WINDOWS[1],
                     jnp.where(grp == 2, POOL_WINDOWS[2], POOL_WINDOWS[3])))
    pos = t_idx * tm + lax.broadcasted_iota(jnp.int32, (tm, 1), 0)
    count = jnp.minimum(pos + 1, wlen).astype(F32)
    pooled = win / count - a
    y_pool = _dot(pooled.astype(BF16), poolw_ref[...]) * pools_ref[...] * _silu(ag_ref[...])

    zc = _history(cc_ref[...] * cx_ref[...], cch_ref[...] * cxh_ref[...], first)
    cw = convw_ref[...]
    conv = cw[2:3] * zc + cw[1:2] * _shift_rows(zc, 1) + cw[0:1] * _shift_rows(zc, 2)
    y_conv = cb_ref[...] * conv[HALO:] * _silu(cg_ref[...])

    branches = ((y_pool, wmp_ref), (yh_ref[...], wmh_ref), (y_conv, wmc_ref), (yd_ref[...], wmd_ref))
    merged = jnp.zeros((tm, d), F32)
    for bi, (y, w_ref) in enumerate(branches):
        gate = _sigmoid(_dot(h, wmg_ref[:, bi * d:(bi + 1) * d]))
        merged = merged + gate * _dot(y.astype(BF16), w_ref[...])
    out = _dot(merged.astype(BF16), wout_ref[...])
    o_ref[...] = x + mod[:, 2 * d:3 * d] * _rms(out, gpost_ref[...])


def _tail(x2, mod_l, g_pre, g_post, p, y_hgrn, y_diff, w_mg, pool_bd, pool_scale, conv_w,
          wm_pool, wm_hgrn, wm_conv, wm_diff, w_out, *, seq, tm):
    m, d = x2.shape
    tiles_per_batch = seq // tm
    hb = tm // HALO
    cw = POOL_WIDTH
    cur = lambda off: pl.BlockSpec((tm, cw), lambda i, o=off: (i, o // cw))
    halo = lambda off: pl.BlockSpec((HALO, cw), lambda i, o=off: (jnp.maximum(i * hb - 1, 0), o // cw))
    whole = lambda a: pl.BlockSpec(a.shape, lambda i, nd=a.ndim: (0,) * nd)
    weights = (w_mg, pool_bd, pool_scale, conv_w, wm_pool, wm_hgrn, wm_conv, wm_diff, w_out)
    return pl.pallas_call(
        functools.partial(_tail_kernel, d=d, tm=tm, tiles_per_batch=tiles_per_batch),
        out_shape=jax.ShapeDtypeStruct((m, d), F32),
        grid=(m // tm,),
        in_specs=[
            pl.BlockSpec((tm, d), lambda i: (i, 0)),
            pl.BlockSpec((1, 1, 3 * d), lambda i: (i // tiles_per_batch, 0, 0)),
            pl.BlockSpec((1, d), lambda i: (0, 0)),
            pl.BlockSpec((1, d), lambda i: (0, 0)),
            cur(OFF_POOL_IN), cur(OFF_POOL_G), halo(OFF_POOL_IN),
            cur(OFF_CX), cur(OFF_CB), cur(OFF_CC), cur(OFF_CG), halo(OFF_CX), halo(OFF_CC),
            pl.BlockSpec((tm, HGRN_WIDTH), lambda i: (i, 0)),
            pl.BlockSpec((tm, DIFF_WIDTH), lambda i: (i, 0)),
        ] + [whole(a) for a in weights],
        out_specs=pl.BlockSpec((tm, d), lambda i: (i, 0)),
        compiler_params=pltpu.CompilerParams(
            dimension_semantics=("parallel",), vmem_limit_bytes=VMEM_LIMIT),
        name="tail",
    )(x2, mod_l, g_pre, g_post, p, p, p, p, p, p, p, p, p, y_hgrn, y_diff, *weights)


def _block_diag(w):
    g, a, b = w.shape
    out = jnp.zeros((g * a, g * b), w.dtype)
    for i in range(g):
        out = out.at[i * a:(i + 1) * a, i * b:(i + 1) * b].set(w[i])
    return out


def kernel(x, c, w_ada, b_ada, g_pre, g_post, w_in, pool_w, pool_scale, hgrn_lb, hgrn_norm, conv_w, diff_lam, diff_norm, w_merge_pool, w_merge_hgrn, w_merge_conv, w_merge_diff, w_out):
    bsz, seq, d = x.shape
    depth = w_ada.shape[0]
    m = bsz * seq
    tm_proj = min(1024, seq)
    tm_tail = min(256, seq)
    tq = min(256, seq)
    tt = min(512, seq)

    mod = _modulation(c, w_ada, b_ada)
    x2 = x.reshape(m, d)
    for l in range(depth):
        mod_l = mod[l].reshape(bsz, 1, 3 * d)
        w_in_l = w_in[l].astype(BF16)
        p = _inproj(x2, mod_l, g_pre[l].reshape(1, d), w_in_l[:, :MIX_COLS],
                    seq=seq, tm=tm_proj, tn=512)
        y_diff = _diff_attention(p, diff_lam[l], diff_norm[l], l, bsz=bsz, seq=seq, tq=tq)
        y_hgrn = _hgrn(p, hgrn_lb, hgrn_norm[l], l, bsz=bsz, seq=seq, tt=tt)
        x2 = _tail(x2, mod_l, g_pre[l].reshape(1, d), g_post[l].reshape(1, d), p, y_hgrn, y_diff,
                   w_in_l[:, MIX_COLS:], _block_diag(pool_w[l]).astype(BF16),
                   pool_scale[l].reshape(1, POOL_WIDTH), conv_w[l],
                   w_merge_pool[l].astype(BF16), w_merge_hgrn[l].astype(BF16),
                   w_merge_conv[l].astype(BF16), w_merge_diff[l].astype(BF16),
                   w_out[l].astype(BF16), seq=seq, tm=tm_tail)
    return x2.reshape(bsz, seq, d)
```

```python
import functools
import math

import numpy as np
import jax
import jax.numpy as jnp
from jax import lax
from jax.experimental import pallas as pl
from jax.experimental.pallas import tpu as pltpu

F32 = jnp.float32
BF16 = jnp.bfloat16

POOL_WIDTH = 256
POOL_WINDOWS = (2, 4, 8, 16)
POOL_GROUP_DIM = 64
HGRN_HEADS = 4
HGRN_DK = 64
HGRN_WIDTH = 256
MIN_F = 1e-20
CONV_WIDTH = 256
CONV_K = 3
DIFF_HEADS = 4
DIFF_DH = 64
DIFF_WIDTH = 512
N_BRANCH = 4
EPS = 1e-6
NEG = -1e30
LOG2E = 1.4426950408889634

OFF_POOL_IN, OFF_POOL_G = 0, 256
OFF_HQ, OFF_HF, OFF_HI, OFF_HG = 512, 768, 1024, 1280
OFF_CX, OFF_CB, OFF_CC, OFF_CG = 1536, 1792, 2048, 2304
OFF_DQ, OFF_DK, OFF_DV, OFF_DG = 2560, 3072, 3584, 4096
MIX_COLS = 4608

LANES = 128
SUBLANES = 8
HALO = 16
HGRN_CHUNK = 64
VMEM_LIMIT = 56 * 1024 * 1024


def _sigmoid(x):
    return 1.0 / (1.0 + jnp.exp(-x))


def _silu(x):
    return x * _sigmoid(x)


def _rms(x, g):
    return x * lax.rsqrt(jnp.mean(x * x, axis=-1, keepdims=True) + EPS) * g


def _dot(a, b):
    return jnp.dot(a, b, preferred_element_type=F32)


def _dot_nt(a, b):
    return lax.dot_general(a, b, (((1,), (1,)), ((), ())), preferred_element_type=F32)


def _split3(x):
    hi = x.astype(BF16)
    r1 = x - hi.astype(F32)
    mid = r1.astype(BF16)
    lo = (r1 - mid.astype(F32)).astype(BF16)
    return hi, mid, lo


def _mod_kernel(c_ref, w_ref, b_ref, o_ref):
    ca = _silu(c_ref[...])
    o_ref[0] = jnp.dot(ca, w_ref[0], preferred_element_type=F32,
                       precision=lax.Precision.HIGHEST) + b_ref[0]


def _modulation(c, w_ada, b_ada):
    depth, d, d3 = w_ada.shape
    bsz = c.shape[0]
    tn = 1024
    return pl.pallas_call(
        _mod_kernel,
        out_shape=jax.ShapeDtypeStruct((depth, bsz, d3), F32),
        grid=(depth, d3 // tn),
        in_specs=[
            pl.BlockSpec((bsz, d), lambda l, j: (0, 0)),
            pl.BlockSpec((1, d, tn), lambda l, j: (l, 0, j)),
            pl.BlockSpec((1, 1, tn), lambda l, j: (l, 0, j)),
        ],
        out_specs=pl.BlockSpec((1, bsz, tn), lambda l, j: (l, 0, j)),
        compiler_params=pltpu.CompilerParams(
            dimension_semantics=("parallel", "parallel"), vmem_limit_bytes=VMEM_LIMIT),
        name="adaln_mod",
    )(c, w_ada, b_ada.reshape(depth, 1, d3))


def _prenorm(x, mod_row, g_pre, d):
    shift = mod_row[:, 0:d]
    scale = mod_row[:, d:2 * d]
    return _rms(x, g_pre) * (1.0 + scale) + shift


def _inproj_kernel(x_ref, mod_ref, g_ref, w_ref, o32_ref, o16_ref, h_ref, *, d, n32_blocks, q_block):
    j = pl.program_id(1)

    @pl.when(j == 0)
    def _():
        h_ref[...] = _prenorm(x_ref[...], mod_ref[0], g_ref[...], d).astype(BF16)

    o = _dot(h_ref[...], w_ref[...])

    @pl.when(j < n32_blocks)
    def _():
        o32_ref[...] = o

    @pl.when(j >= n32_blocks)
    def _():
        o16_ref[...] = (o * jnp.where(j == q_block, DIFF_DH ** -0.5 * LOG2E, 1.0)).astype(BF16)


def _inproj(x2, mod_l, g_pre, w_mix, *, seq, tm, tn):
    m, d = x2.shape
    n = w_mix.shape[1]
    assert tn == DIFF_WIDTH and OFF_DQ % tn == 0 and n % tn == 0
    n32_blocks = OFF_DQ // tn
    tiles_per_batch = seq // tm
    return pl.pallas_call(
        functools.partial(_inproj_kernel, d=d, n32_blocks=n32_blocks, q_block=n32_blocks),
        out_shape=(jax.ShapeDtypeStruct((m, OFF_DQ), F32),
                   jax.ShapeDtypeStruct((m, n - OFF_DQ), BF16)),
        grid=(m // tm, n // tn),
        in_specs=[
            pl.BlockSpec((tm, d), lambda i, j: (i, 0)),
            pl.BlockSpec((1, 1, 3 * d), lambda i, j: (i // tiles_per_batch, 0, 0)),
            pl.BlockSpec((1, d), lambda i, j: (0, 0)),
            pl.BlockSpec((d, tn), lambda i, j: (0, j)),
        ],
        out_specs=(pl.BlockSpec((tm, tn), lambda i, j: (i, jnp.minimum(j, n32_blocks - 1))),
                   pl.BlockSpec((tm, tn), lambda i, j: (i, jnp.maximum(j - n32_blocks, 0)))),
        scratch_shapes=[pltpu.VMEM((tm, d), BF16)],
        compiler_params=pltpu.CompilerParams(
            dimension_semantics=("parallel", "arbitrary"), vmem_limit_bytes=VMEM_LIMIT),
        name="in_proj",
    )(x2, mod_l, g_pre, w_mix)


def _attn_kernel(q_ref, k_ref, v_ref, g_ref, lam_ref, ng_ref, o_ref, s_ref, *, tq, tk, unroll, lam_init):
    i = pl.program_id(2)
    sub = tq // tk
    n_full = i * sub
    lane = lax.broadcasted_iota(jnp.int32, (1, LANES), 1)
    q = q_ref[...]
    zero = jnp.zeros_like(q)
    q_maps = (jnp.where(lane < DIFF_DH, q, zero), jnp.where(lane >= DIFF_DH, q, zero))

    def lane_fold(x, op):
        out = x[:, 0:LANES]
        for cb in range(1, tk // LANES):
            out = op(out, x[:, cb * LANES:(cb + 1) * LANES])
        return out

    def scores(j, mx, diag):
        kj = k_ref[pl.ds(pl.multiple_of(j * tk, tk), tk), :]
        if diag is not None:
            row = lax.broadcasted_iota(jnp.int32, (tq, tk), 0)
            col = lax.broadcasted_iota(jnp.int32, (tq, tk), 1) + diag * tk
            keep = col <= row
        out = []
        for mp in range(2):
            s = _dot_nt(q_maps[mp], kj)
            if diag is not None:
                s = jnp.where(keep, s, NEG)
            s_ref[mp, j] = s
            out.append(jnp.maximum(mx[mp], lane_fold(s, jnp.maximum)))
        return tuple(out)

    def blocked_loop(n, body, carry):
        def group(t, c):
            for u in range(unroll):
                c = body(t * unroll + u, c)
            return c
        carry = lax.fori_loop(0, n // unroll, group, carry)
        return lax.fori_loop((n // unroll) * unroll, n, body, carry)

    mx = (jnp.full((tq, LANES), NEG, F32),) * 2
    mx = blocked_loop(n_full, lambda j, c: scores(j, c, None), mx)
    for dg in range(sub):
        mx = scores(n_full + dg, mx, dg)
    m_row = tuple(jnp.max(v, axis=-1, keepdims=True) for v in mx)

    def accumulate(j, carry):
        vj = v_ref[pl.ds(pl.multiple_of(j * tk, tk), tk), :]
        out = []
        for mp in range(2):
            l_acc, acc = carry[mp]
            p = jnp.exp2(s_ref[mp, j] - m_row[mp])
            out.append((l_acc + lane_fold(p, jnp.add), acc + _dot(p.astype(BF16), vj)))
        return tuple(out)

    init = ((jnp.zeros((tq, LANES), F32), jnp.zeros((tq, LANES), F32)),) * 2
    (l0, a0), (l1, a1) = blocked_loop(n_full + sub, accumulate, init)
    l0 = jnp.sum(l0, axis=-1, keepdims=True)
    l1 = jnp.sum(l1, axis=-1, keepdims=True)

    lf = lam_ref[...]
    lam_full = (jnp.exp(jnp.sum(lf[0:1] * lf[1:2], axis=-1, keepdims=True))
                - jnp.exp(jnp.sum(lf[2:3] * lf[3:4], axis=-1, keepdims=True)) + lam_init)
    o = a0 / l0 - lam_full * (a1 / l1)
    o = _rms(o, ng_ref[...]) * (1.0 - lam_init)
    o_ref[...] = o * _silu(g_ref[...].astype(F32))


def _diff_attention(p16, lam, norm_g, layer_idx, *, bsz, seq, tq, tk):
    m = p16.shape[0]
    nq = seq // tq
    lam_init = 0.8 - 0.6 * math.exp(-0.3 * layer_idx)
    cq, ck, cv, cg = ((off - OFF_DQ) // LANES for off in (OFF_DQ, OFF_DK, OFF_DV, OFF_DG))
    return pl.pallas_call(
        functools.partial(_attn_kernel, tq=tq, tk=tk, unroll=4, lam_init=lam_init),
        out_shape=jax.ShapeDtypeStruct((m, DIFF_WIDTH), F32),
        grid=(bsz, DIFF_HEADS, nq),
        in_specs=[
            pl.BlockSpec((tq, LANES), lambda b, h, i: (b * nq + i, cq + h)),
            pl.BlockSpec((seq, LANES), lambda b, h, i: (b, ck + h)),
            pl.BlockSpec((seq, LANES), lambda b, h, i: (b, cv + h)),
            pl.BlockSpec((tq, LANES), lambda b, h, i: (b * nq + i, cg + h)),
            pl.BlockSpec((4, DIFF_DH), lambda b, h, i: (0, 0)),
            pl.BlockSpec((1, LANES), lambda b, h, i: (0, 0)),
        ],
        out_specs=pl.BlockSpec((tq, LANES), lambda b, h, i: (b * nq + i, h)),
        scratch_shapes=[pltpu.VMEM((2, seq // tk, tq, tk), F32)],
        compiler_params=pltpu.CompilerParams(
            dimension_semantics=("parallel", "parallel", "arbitrary"),
            vmem_limit_bytes=VMEM_LIMIT),
        name="diff_attn",
    )(p16, p16, p16, p16, lam, norm_g.reshape(1, LANES))


def _hgrn_constants():
    c = HGRN_CHUNK
    t = np.arange(c)
    mats = [t[None, :] <= t[:, None], t[None, :] > t[:, None]]
    q_mats, k_mats, masks = [], [], [np.eye(c, dtype=bool)]
    m = 1
    while m < c:
        blk = t // m
        odd = blk % 2 == 1
        q_mats.append(odd[:, None] & (t[None, :] >= (m * blk)[:, None]) & (t[None, :] <= t[:, None]))
        k_mats.append((~odd)[:, None] & (t[None, :] > t[:, None])
                      & (t[None, :] <= (m * (blk + 1) - 1)[:, None]))
        masks.append(odd[:, None] & (blk[None, :] == (blk - 1)[:, None]))
        m *= 2
    wsum = np.concatenate(mats + q_mats + k_mats, axis=0).astype(np.float32)
    masks = np.stack([np.tile(mk, (1, HGRN_HEADS)) for mk in masks]).astype(np.float32)
    head = np.arange(HGRN_WIDTH) // HGRN_DK
    bmask = (head[:, None] == head[None, :]).astype(np.float32)
    return wsum, masks, bmask


def _hgrn_kernel(q_ref, f_ref, i_ref, g_ref, lb_ref, ng_ref, wsum_ref, masks_ref, bmask_ref,
                 o_ref, st_ref, *, layer_idx, n_chunks):
    c = HGRN_CHUNK
    n_levels = masks_ref.shape[0] - 1

    @pl.when(pl.program_id(1) == 0)
    def _():
        st_ref[...] = jnp.zeros_like(st_ref)

    lb_all = lb_ref[...]
    e = jnp.exp(lb_all - jnp.max(lb_all, axis=0, keepdims=True))
    sm = e / jnp.sum(e, axis=0, keepdims=True)
    lb = sm[0:1]
    for r in range(1, layer_idx + 1):
        lb = lb + sm[r:r + 1]
    lb = lb - sm[0:1]

    bmask = bmask_ref[...]
    wsum = wsum_ref[...]
    ng = ng_ref[...]

    def chunk(ci, _):
        rows = pl.ds(pl.multiple_of(ci * c, c), c)
        z = f_ref[rows, :]
        v = i_ref[rows, :]
        q = q_ref[rows, :]
        f = lb + (1.0 - lb) * _sigmoid(z)
        g = jnp.log(jnp.maximum(f, MIN_F))
        kk = (1.0 - lb) * _sigmoid(-z)
        qf = _silu(q)

        d = sum(_dot(wsum, part) for part in _split3(g))
        b = d[0:c]
        b_rest = d[c:2 * c]
        b_last = b[c - 1:c]

        vb = v.astype(BF16)
        v_bd = (jnp.concatenate([v] * HGRN_HEADS, axis=0) * bmask).astype(BF16)

        a_cat = jnp.zeros((c, HGRN_WIDTH), F32)
        for lv in range(n_levels + 1):
            if lv == 0:
                qs, ks = qf, kk
            else:
                dq = d[(1 + lv) * c:(2 + lv) * c]
                dk = d[(1 + n_levels + lv) * c:(2 + n_levels + lv) * c]
                qs, ks = qf * jnp.exp(dq), kk * jnp.exp(dk)
            ks_bd = (jnp.concatenate([ks] * HGRN_HEADS, axis=0) * bmask).astype(BF16)
            a_cat = a_cat + masks_ref[lv] * _dot_nt(qs.astype(BF16), ks_bd)

        st = st_ref[...]
        o = _dot_nt((qf * jnp.exp(b)).astype(BF16), st.astype(BF16)) + _dot(a_cat.astype(BF16), v_bd)
        kd = (kk * jnp.exp(b_rest)).astype(BF16)
        st_ref[...] = st * jnp.exp(b_last) + bmask * _dot(vb.T, kd)

        ms = _dot(o * o, bmask * (1.0 / HGRN_DK))
        y = o * lax.rsqrt(ms + EPS) * ng
        o_ref[rows, :] = y * _silu(g_ref[rows, :])
        return 0

    lax.fori_loop(0, n_chunks, chunk, 0)


def _hgrn(p, hgrn_lb, norm_g, layer_idx, *, bsz, seq, tt):
    m = p.shape[0]
    nt = seq // tt
    wsum, masks, bmask = _hgrn_constants()
    w = HGRN_WIDTH
    col = lambda off: off // w
    spec = lambda off: pl.BlockSpec((tt, w), lambda b, t, o=off: (b * nt + t, col(o)))
    whole = lambda a: pl.BlockSpec(a.shape, lambda b, t, nd=a.ndim: (0,) * nd)
    ng = jnp.tile(norm_g.reshape(1, HGRN_DK), (1, HGRN_HEADS))
    consts = (hgrn_lb, ng, jnp.asarray(wsum, BF16), jnp.asarray(masks), jnp.asarray(bmask))
    return pl.pallas_call(
        functools.partial(_hgrn_kernel, layer_idx=layer_idx, n_chunks=tt // HGRN_CHUNK),
        out_shape=jax.ShapeDtypeStruct((m, w), F32),
        grid=(bsz, nt),
        in_specs=[spec(OFF_HQ), spec(OFF_HF), spec(OFF_HI), spec(OFF_HG)] + [whole(a) for a in consts],
        out_specs=pl.BlockSpec((tt, w), lambda b, t: (b * nt + t, 0)),
        scratch_shapes=[pltpu.VMEM((w, w), F32)],
        compiler_params=pltpu.CompilerParams(
            dimension_semantics=("parallel", "arbitrary"), vmem_limit_bytes=VMEM_LIMIT),
        name="hgrn2",
    )(p, p, p, p, *consts)


def _history(cur, halo, first):
    return jnp.concatenate([jnp.where(first, 0.0, halo), cur], axis=0)


def _shift_rows(x, k):
    return pltpu.roll(x, k, 0)


def _tail_kernel(x_ref, mod_ref, gpre_ref, gpost_ref,
                 ain_ref, ag_ref, ainh_ref, cx_ref, cb_ref, cc_ref, cg_ref, cxh_ref, cch_ref,
                 yh_ref, yd_ref,
                 wmg_ref, poolw_ref, pools_ref, convw_ref,
                 wmp_ref, wmh_ref, wmc_ref, wmd_ref, wout_ref,
                 o_ref, *, d, tm, tiles_per_batch):
    t_idx = pl.program_id(0) % tiles_per_batch
    first = t_idx == 0
    x = x_ref[...]
    mod = mod_ref[0]
    h = _prenorm(x, mod, gpre_ref[...], d).astype(BF16)

    a = ain_ref[...]
    s = _history(a, ainh_ref[...], first)
    lane = lax.broadcasted_iota(jnp.int32, (1, POOL_WIDTH), 1)
    grp = lane // POOL_GROUP_DIM
    win = None
    for gi, w in enumerate(POOL_WINDOWS):
        s = s + _shift_rows(s, w // 2)
        win = s if win is None else jnp.where(grp >= gi, s, win)
    win = win[HALO:]
    wlen = jnp.where(grp == 0, POOL_WINDOWS[0], jnp.where(grp == 1, POOL_WINDOWS[1],
                     jnp.where(grp == 2, POOL_WINDOWS[2], POOL_WINDOWS[3])))
    pos = t_idx * tm + lax.broadcasted_iota(jnp.int32, (tm, 1), 0)
    count = jnp.minimum(pos + 1, wlen).astype(F32)
    pooled = win / count - a
    y_pool = _dot(pooled.astype(BF16), poolw_ref[...]) * pools_ref[...] * _silu(ag_ref[...])

    zc = _history(cc_ref[...] * cx_ref[...], cch_ref[...] * cxh_ref[...], first)
    cw = convw_ref[...]
    conv = cw[2:3] * zc + cw[1:2] * _shift_rows(zc, 1) + cw[0:1] * _shift_rows(zc, 2)
    y_conv = cb_ref[...] * conv[HALO:] * _silu(cg_ref[...])

    branches = ((y_pool, wmp_ref), (yh_ref[...], wmh_ref), (y_conv, wmc_ref), (yd_ref[...], wmd_ref))
    merged = jnp.zeros((tm, d), F32)
    for bi, (y, w_ref) in enumerate(branches):
        gate = _sigmoid(_dot(h, wmg_ref[:, bi * d:(bi + 1) * d]))
        merged = merged + gate * _dot(y.astype(BF16), w_ref[...])
    out = _dot(merged.astype(BF16), wout_ref[...])
    o_ref[...] = x + mod[:, 2 * d:3 * d] * _rms(out, gpost_ref[...])


def _tail(x2, mod_l, g_pre, g_post, p, y_hgrn, y_diff, w_mg, pool_bd, pool_scale, conv_w,
          wm_pool, wm_hgrn, wm_conv, wm_diff, w_out, *, seq, tm):
    m, d = x2.shape
    tiles_per_batch = seq // tm
    hb = tm // HALO
    cw = POOL_WIDTH
    cur = lambda off: pl.BlockSpec((tm, cw), lambda i, o=off: (i, o // cw))
    halo = lambda off: pl.BlockSpec((HALO, cw), lambda i, o=off: (jnp.maximum(i * hb - 1, 0), o // cw))
    whole = lambda a: pl.BlockSpec(a.shape, lambda i, nd=a.ndim: (0,) * nd)
    weights = (w_mg, pool_bd, pool_scale, conv_w, wm_pool, wm_hgrn, wm_conv, wm_diff, w_out)
    return pl.pallas_call(
        functools.partial(_tail_kernel, d=d, tm=tm, tiles_per_batch=tiles_per_batch),
        out_shape=jax.ShapeDtypeStruct((m, d), F32),
        grid=(m // tm,),
        in_specs=[
            pl.BlockSpec((tm, d), lambda i: (i, 0)),
            pl.BlockSpec((1, 1, 3 * d), lambda i: (i // tiles_per_batch, 0, 0)),
            pl.BlockSpec((1, d), lambda i: (0, 0)),
            pl.BlockSpec((1, d), lambda i: (0, 0)),
            cur(OFF_POOL_IN), cur(OFF_POOL_G), halo(OFF_POOL_IN),
            cur(OFF_CX), cur(OFF_CB), cur(OFF_CC), cur(OFF_CG), halo(OFF_CX), halo(OFF_CC),
            pl.BlockSpec((tm, HGRN_WIDTH), lambda i: (i, 0)),
            pl.BlockSpec((tm, DIFF_WIDTH), lambda i: (i, 0)),
        ] + [whole(a) for a in weights],
        out_specs=pl.BlockSpec((tm, d), lambda i: (i, 0)),
        compiler_params=pltpu.CompilerParams(
            dimension_semantics=("parallel",), vmem_limit_bytes=VMEM_LIMIT),
        name="tail",
    )(x2, mod_l, g_pre, g_post, p, p, p, p, p, p, p, p, p, y_hgrn, y_diff, *weights)


def _block_diag(w):
    g, a, b = w.shape
    out = jnp.zeros((g * a, g * b), w.dtype)
    for i in range(g):
        out = out.at[i * a:(i + 1) * a, i * b:(i + 1) * b].set(w[i])
    return out


def kernel(x, c, w_ada, b_ada, g_pre, g_post, w_in, pool_w, pool_scale, hgrn_lb, hgrn_norm, conv_w, diff_lam, diff_norm, w_merge_pool, w_merge_hgrn, w_merge_conv, w_merge_diff, w_out):
    bsz, seq, d = x.shape
    depth = w_ada.shape[0]
    m = bsz * seq
    tm_proj = min(1024, seq)
    tm_tail = min(256, seq)
    tq = min(512, seq)
    tk = min(512, seq)
    tt = min(512, seq)

    mod = _modulation(c, w_ada, b_ada)
    x2 = x.reshape(m, d)
    for l in range(depth):
        mod_l = mod[l].reshape(bsz, 1, 3 * d)
        w_in_l = w_in[l].astype(BF16)
        p, p16 = _inproj(x2, mod_l, g_pre[l].reshape(1, d), w_in_l[:, :MIX_COLS],
                         seq=seq, tm=tm_proj, tn=DIFF_WIDTH)
        y_diff = _diff_attention(p16, diff_lam[l], diff_norm[l], l, bsz=bsz, seq=seq, tq=tq, tk=tk)
        y_hgrn = _hgrn(p, hgrn_lb, hgrn_norm[l], l, bsz=bsz, seq=seq, tt=tt)
        x2 = _tail(x2, mod_l, g_pre[l].reshape(1, d), g_post[l].reshape(1, d), p, y_hgrn, y_diff,
                   w_in_l[:, MIX_COLS:], _block_diag(pool_w[l]).astype(BF16),
                   pool_scale[l].reshape(1, POOL_WIDTH), conv_w[l],
                   w_merge_pool[l].astype(BF16), w_merge_hgrn[l].astype(BF16),
                   w_merge_conv[l].astype(BF16), w_merge_diff[l].astype(BF16),
                   w_out[l].astype(BF16), seq=seq, tm=tm_tail)
    return x2.reshape(bsz, seq, d)
```

```python
import functools
import math

import numpy as np
import jax
import jax.numpy as jnp
from jax import lax
from jax.experimental import pallas as pl
from jax.experimental.pallas import tpu as pltpu

F32 = jnp.float32
BF16 = jnp.bfloat16

POOL_WIDTH = 256
POOL_WINDOWS = (2, 4, 8, 16)
POOL_GROUP_DIM = 64
HGRN_HEADS = 4
HGRN_DK = 64
HGRN_WIDTH = 256
MIN_F = 1e-20
CONV_WIDTH = 256
CONV_K = 3
DIFF_HEADS = 4
DIFF_DH = 64
DIFF_WIDTH = 512
N_BRANCH = 4
EPS = 1e-6
NEG = -1e30
LOG2E = 1.4426950408889634

OFF_POOL_IN, OFF_POOL_G = 0, 256
OFF_HQ, OFF_HF, OFF_HI, OFF_HG = 512, 768, 1024, 1280
OFF_CX, OFF_CB, OFF_CC, OFF_CG = 1536, 1792, 2048, 2304
OFF_DQ, OFF_DK, OFF_DV, OFF_DG = 2560, 3072, 3584, 4096
MIX_COLS = 4608

LANES = 128
SUBLANES = 8
HALO = 16
HGRN_CHUNK = 64
VMEM_LIMIT = 56 * 1024 * 1024


def _sigmoid(x):
    return 1.0 / (1.0 + jnp.exp(-x))


def _silu(x):
    return x * _sigmoid(x)


def _rms(x, g):
    return x * lax.rsqrt(jnp.mean(x * x, axis=-1, keepdims=True) + EPS) * g


def _dot(a, b):
    return jnp.dot(a, b, preferred_element_type=F32)


def _dot_nt(a, b):
    return lax.dot_general(a, b, (((1,), (1,)), ((), ())), preferred_element_type=F32)


def _split3(x):
    hi = x.astype(BF16)
    r1 = x - hi.astype(F32)
    mid = r1.astype(BF16)
    lo = (r1 - mid.astype(F32)).astype(BF16)
    return hi, mid, lo


def _mod_kernel(c_ref, w_ref, b_ref, o_ref):
    ca = _silu(c_ref[...])
    o_ref[0] = jnp.dot(ca, w_ref[0], preferred_element_type=F32,
                       precision=lax.Precision.HIGHEST) + b_ref[0]


def _modulation(c, w_ada, b_ada):
    depth, d, d3 = w_ada.shape
    bsz = c.shape[0]
    tn = 1024
    return pl.pallas_call(
        _mod_kernel,
        out_shape=jax.ShapeDtypeStruct((depth, bsz, d3), F32),
        grid=(depth, d3 // tn),
        in_specs=[
            pl.BlockSpec((bsz, d), lambda l, j: (0, 0)),
            pl.BlockSpec((1, d, tn), lambda l, j: (l, 0, j)),
            pl.BlockSpec((1, 1, tn), lambda l, j: (l, 0, j)),
        ],
        out_specs=pl.BlockSpec((1, bsz, tn), lambda l, j: (l, 0, j)),
        compiler_params=pltpu.CompilerParams(
            dimension_semantics=("parallel", "parallel"), vmem_limit_bytes=VMEM_LIMIT),
        name="adaln_mod",
    )(c, w_ada, b_ada.reshape(depth, 1, d3))


def _prenorm(x, mod_row, g_pre, d):
    shift = mod_row[:, 0:d]
    scale = mod_row[:, d:2 * d]
    return _rms(x, g_pre) * (1.0 + scale) + shift


def _inproj_kernel(x_ref, mod_ref, g_ref, w_ref, o32_ref, o16_ref, *, d, tn):
    h = _prenorm(x_ref[...], mod_ref[0], g_ref[...], d).astype(BF16)
    for n0 in range(0, MIX_COLS, tn):
        o = _dot(h, w_ref[:, n0:n0 + tn])
        if n0 < OFF_DQ:
            o32_ref[:, n0:n0 + tn] = o
        else:
            if n0 == OFF_DQ:
                o = o * (DIFF_DH ** -0.5 * LOG2E)
            o16_ref[:, n0 - OFF_DQ:n0 - OFF_DQ + tn] = o.astype(BF16)


def _inproj(x2, mod_l, g_pre, w_mix, *, seq, tm, tn):
    m, d = x2.shape
    n = w_mix.shape[1]
    assert tn == DIFF_WIDTH and OFF_DQ % tn == 0 and n == MIX_COLS
    tiles_per_batch = seq // tm
    return pl.pallas_call(
        functools.partial(_inproj_kernel, d=d, tn=tn),
        out_shape=(jax.ShapeDtypeStruct((m, OFF_DQ), F32),
                   jax.ShapeDtypeStruct((m, n - OFF_DQ), BF16)),
        grid=(m // tm,),
        in_specs=[
            pl.BlockSpec((tm, d), lambda i: (i, 0)),
            pl.BlockSpec((1, 1, 3 * d), lambda i: (i // tiles_per_batch, 0, 0)),
            pl.BlockSpec((1, d), lambda i: (0, 0)),
            pl.BlockSpec((d, n), lambda i: (0, 0)),
        ],
        out_specs=(pl.BlockSpec((tm, OFF_DQ), lambda i: (i, 0)),
                   pl.BlockSpec((tm, n - OFF_DQ), lambda i: (i, 0))),
        compiler_params=pltpu.CompilerParams(
            dimension_semantics=("parallel",), vmem_limit_bytes=VMEM_LIMIT),
        name="in_proj",
    )(x2, mod_l, g_pre, w_mix)


def _attn_kernel(q_ref, k_ref, v_ref, g_ref, lam_ref, ng_ref, o_ref, s_ref, *, tq, tk, unroll, lam_init):
    i = pl.program_id(2)
    sub = tq // tk
    n_full = i * sub
    lane = lax.broadcasted_iota(jnp.int32, (1, LANES), 1)
    q = q_ref[...]
    zero = jnp.zeros_like(q)
    q_maps = (jnp.where(lane < DIFF_DH, q, zero), jnp.where(lane >= DIFF_DH, q, zero))

    def lane_fold(x, op):
        out = x[:, 0:LANES]
        for cb in range(1, tk // LANES):
            out = op(out, x[:, cb * LANES:(cb + 1) * LANES])
        return out

    def scores(j, mx, diag):
        kj = k_ref[pl.ds(pl.multiple_of(j * tk, tk), tk), :]
        if diag is not None:
            row = lax.broadcasted_iota(jnp.int32, (tq, tk), 0)
            col = lax.broadcasted_iota(jnp.int32, (tq, tk), 1) + diag * tk
            keep = col <= row
        out = []
        for mp in range(2):
            s = _dot_nt(q_maps[mp], kj)
            if diag is not None:
                s = jnp.where(keep, s, NEG)
            s_ref[mp, j] = s
            out.append(jnp.maximum(mx[mp], lane_fold(s, jnp.maximum)))
        return tuple(out)

    def blocked_loop(n, body, carry):
        def group(t, c):
            for u in range(unroll):
                c = body(t * unroll + u, c)
            return c
        carry = lax.fori_loop(0, n // unroll, group, carry)
        return lax.fori_loop((n // unroll) * unroll, n, body, carry)

    mx = (jnp.full((tq, LANES), NEG, F32),) * 2
    mx = blocked_loop(n_full, lambda j, c: scores(j, c, None), mx)
    for dg in range(sub):
        mx = scores(n_full + dg, mx, dg)
    m_row = tuple(jnp.max(v, axis=-1, keepdims=True) for v in mx)

    def accumulate(j, carry):
        vj = v_ref[pl.ds(pl.multiple_of(j * tk, tk), tk), :]
        out = []
        for mp in range(2):
            l_acc, acc = carry[mp]
            p = jnp.exp2(s_ref[mp, j] - m_row[mp])
            out.append((l_acc + lane_fold(p, jnp.add), acc + _dot(p.astype(BF16), vj)))
        return tuple(out)

    init = ((jnp.zeros((tq, LANES), F32), jnp.zeros((tq, LANES), F32)),) * 2
    (l0, a0), (l1, a1) = blocked_loop(n_full + sub, accumulate, init)
    l0 = jnp.sum(l0, axis=-1, keepdims=True)
    l1 = jnp.sum(l1, axis=-1, keepdims=True)

    lf = lam_ref[...]
    lam_full = (jnp.exp(jnp.sum(lf[0:1] * lf[1:2], axis=-1, keepdims=True))
                - jnp.exp(jnp.sum(lf[2:3] * lf[3:4], axis=-1, keepdims=True)) + lam_init)
    o = a0 / l0 - lam_full * (a1 / l1)
    o = _rms(o, ng_ref[...]) * (1.0 - lam_init)
    o_ref[...] = o * _silu(g_ref[...].astype(F32))


def _diff_attention(p16, lam, norm_g, layer_idx, *, bsz, seq, tq, tk):
    m = p16.shape[0]
    nq = seq // tq
    lam_init = 0.8 - 0.6 * math.exp(-0.3 * layer_idx)
    cq, ck, cv, cg = ((off - OFF_DQ) // LANES for off in (OFF_DQ, OFF_DK, OFF_DV, OFF_DG))
    return pl.pallas_call(
        functools.partial(_attn_kernel, tq=tq, tk=tk, unroll=4, lam_init=lam_init),
        out_shape=jax.ShapeDtypeStruct((m, DIFF_WIDTH), F32),
        grid=(bsz, DIFF_HEADS, nq),
        in_specs=[
            pl.BlockSpec((tq, LANES), lambda b, h, i: (b * nq + i, cq + h)),
            pl.BlockSpec((seq, LANES), lambda b, h, i: (b, ck + h)),
            pl.BlockSpec((seq, LANES), lambda b, h, i: (b, cv + h)),
            pl.BlockSpec((tq, LANES), lambda b, h, i: (b * nq + i, cg + h)),
            pl.BlockSpec((4, DIFF_DH), lambda b, h, i: (0, 0)),
            pl.BlockSpec((1, LANES), lambda b, h, i: (0, 0)),
        ],
        out_specs=pl.BlockSpec((tq, LANES), lambda b, h, i: (b * nq + i, h)),
        scratch_shapes=[pltpu.VMEM((2, seq // tk, tq, tk), F32)],
        compiler_params=pltpu.CompilerParams(
            dimension_semantics=("parallel", "parallel", "arbitrary"),
            vmem_limit_bytes=VMEM_LIMIT),
        name="diff_attn",
    )(p16, p16, p16, p16, lam, norm_g.reshape(1, LANES))


def _hgrn_constants():
    c = HGRN_CHUNK
    t = np.arange(c)
    sizes = [c >> k for k in range(int(math.log2(c)))]
    small = [m for m in sizes if m < SUBLANES]
    prefix = [((t[None, :] // m) == (t[:, None] // m)) & (t[None, :] <= t[:, None]) for m in sizes]
    suffix = [((t[None, :] // m) == (t[:, None] // m)) & (t[None, :] > t[:, None]) for m in small]
    masks = [np.eye(c, dtype=bool)]
    m = 1
    while m < c:
        blk = t // m
        masks.append((blk % 2 == 1)[:, None] & (blk[None, :] == (blk - 1)[:, None]))
        m *= 2
    wsum = np.concatenate(prefix + suffix, axis=0).astype(np.float32)
    masks = np.stack([np.tile(mk, (1, HGRN_HEADS)) for mk in masks]).astype(np.float32)
    head = np.arange(HGRN_WIDTH) // HGRN_DK
    bmask = (head[:, None] == head[None, :]).astype(np.float32)
    return wsum, masks, bmask, sizes, small


def _block_end(x, m):
    n, w = x.shape
    x3 = x.reshape(n // m, m, w)
    return jnp.broadcast_to(x3[:, m - 1:m, :], x3.shape).reshape(n, w)


def _hgrn_kernel(q_ref, f_ref, i_ref, g_ref, lb_ref, ng_ref, wsum_ref, masks_ref, bmask_ref,
                 o_ref, st_ref, *, layer_idx, n_chunks, sizes, small):
    c = HGRN_CHUNK

    @pl.when(pl.program_id(1) == 0)
    def _():
        st_ref[...] = jnp.zeros_like(st_ref)

    lb_all = lb_ref[...]
    e = jnp.exp(lb_all - jnp.max(lb_all, axis=0, keepdims=True))
    sm = e / jnp.sum(e, axis=0, keepdims=True)
    lb = sm[0:1]
    for r in range(1, layer_idx + 1):
        lb = lb + sm[r:r + 1]
    lb = lb - sm[0:1]

    bmask = bmask_ref[...]
    bmask16 = bmask.astype(BF16)
    wsum = wsum_ref[...]
    chunks = [slice(ci * c, (ci + 1) * c) for ci in range(n_chunks)]

    z = f_ref[...]
    fc = jnp.maximum(lb + (1.0 - lb) * _sigmoid(z), MIN_F)
    g = jnp.log(fc)
    kk = (1.0 - lb) * _sigmoid(-z)
    qf = _silu(q_ref[...])
    v16 = i_ref[...].astype(BF16)

    g_hi = g.astype(BF16)
    g_lo = (g - g_hi.astype(F32)).astype(BF16)
    d = [_dot(wsum, g_hi[rows]) + _dot(wsum, g_lo[rows]) for rows in chunks]
    nsz = len(sizes)
    pre = {m: jnp.concatenate([dc[k * c:(k + 1) * c] for dc in d], axis=0) for k, m in enumerate(sizes)}
    suf = {m: jnp.concatenate([dc[(nsz + k) * c:(nsz + k + 1) * c] for dc in d], axis=0)
           for k, m in enumerate(small)}
    for m in sizes:
        if m not in suf:
            suf[m] = _block_end(pre[m], m) - pre[m]

    sides = [(qf, kk), (qf * fc, kk)]
    for m in reversed(sizes[1:]):
        sides.append((qf * jnp.exp(pre[m]), kk * jnp.exp(suf[m])))
    sides = [(qs.astype(BF16), ks.astype(BF16)) for qs, ks in sides]
    b = pre[c]
    eb_last = jnp.exp(b)
    bq = (qf * eb_last).astype(BF16)
    kd = (kk * jnp.exp(suf[c])).astype(BF16)

    def block_diag(x):
        return jnp.concatenate([x] * HGRN_HEADS, axis=0) * bmask16

    o_intra = []
    for rows in chunks:
        a_cat = None
        for lv, (qs, ks) in enumerate(sides):
            term = masks_ref[lv] * _dot_nt(qs[rows], block_diag(ks[rows]))
            a_cat = term if a_cat is None else a_cat + term
        o_intra.append(_dot(a_cat.astype(BF16), block_diag(v16[rows])))

    st = st_ref[...]
    outs = []
    for ci, rows in enumerate(chunks):
        outs.append(o_intra[ci] + _dot_nt(bq[rows], st.astype(BF16)))
        last = (ci + 1) * c - 1
        st = st * eb_last[last:last + 1] + bmask * _dot(v16[rows].T, kd[rows])
    st_ref[...] = st

    o = jnp.concatenate(outs, axis=0)
    ms = _dot(o * o, bmask * (1.0 / HGRN_DK))
    y = o * lax.rsqrt(ms + EPS) * ng_ref[...]
    o_ref[...] = y * _silu(g_ref[...])


def _hgrn(p, hgrn_lb, norm_g, layer_idx, *, bsz, seq, tt):
    m = p.shape[0]
    nt = seq // tt
    wsum, masks, bmask, sizes, small = _hgrn_constants()
    w = HGRN_WIDTH
    col = lambda off: off // w
    spec = lambda off: pl.BlockSpec((tt, w), lambda b, t, o=off: (b * nt + t, col(o)))
    whole = lambda a: pl.BlockSpec(a.shape, lambda b, t, nd=a.ndim: (0,) * nd)
    ng = jnp.tile(norm_g.reshape(1, HGRN_DK), (1, HGRN_HEADS))
    consts = (hgrn_lb, ng, jnp.asarray(wsum, BF16), jnp.asarray(masks), jnp.asarray(bmask))
    return pl.pallas_call(
        functools.partial(_hgrn_kernel, layer_idx=layer_idx, n_chunks=tt // HGRN_CHUNK,
                          sizes=tuple(sizes), small=tuple(small)),
        out_shape=jax.ShapeDtypeStruct((m, w), F32),
        grid=(bsz, nt),
        in_specs=[spec(OFF_HQ), spec(OFF_HF), spec(OFF_HI), spec(OFF_HG)] + [whole(a) for a in consts],
        out_specs=pl.BlockSpec((tt, w), lambda b, t: (b * nt + t, 0)),
        scratch_shapes=[pltpu.VMEM((w, w), F32)],
        compiler_params=pltpu.CompilerParams(
            dimension_semantics=("parallel", "arbitrary"), vmem_limit_bytes=VMEM_LIMIT),
        name="hgrn2",
    )(p, p, p, p, *consts)


def _history(cur, halo, first):
    return jnp.concatenate([jnp.where(first, 0.0, halo), cur], axis=0)


def _shift_rows(x, k):
    return pltpu.roll(x, k, 0)


def _tail_kernel(x_ref, mod_ref, gpre_ref, gpost_ref,
                 ain_ref, ag_ref, ainh_ref, cx_ref, cb_ref, cc_ref, cg_ref, cxh_ref, cch_ref,
                 yh_ref, yd_ref,
                 wmg_ref, poolw_ref, pools_ref, convw_ref,
                 wmp_ref, wmh_ref, wmc_ref, wmd_ref, wout_ref,
                 o_ref, *, d, tm, tiles_per_batch):
    t_idx = pl.program_id(0) % tiles_per_batch
    first = t_idx == 0
    x = x_ref[...]
    mod = mod_ref[0]
    h = _prenorm(x, mod, gpre_ref[...], d).astype(BF16)

    a = ain_ref[...]
    s = _history(a, ainh_ref[...], first)
    lane = lax.broadcasted_iota(jnp.int32, (1, POOL_WIDTH), 1)
    grp = lane // POOL_GROUP_DIM
    win = None
    for gi, w in enumerate(POOL_WINDOWS):
        s = s + _shift_rows(s, w // 2)
        win = s if win is None else jnp.where(grp >= gi, s, win)
    win = win[HALO:]
    wlen = jnp.where(grp == 0, POOL_WINDOWS[0], jnp.where(grp == 1, POOL_WINDOWS[1],
                     jnp.where(grp == 2, POOL_WINDOWS[2], POOL_WINDOWS[3])))
    pos = t_idx * tm + lax.broadcasted_iota(jnp.int32, (tm, 1), 0)
    count = jnp.minimum(pos + 1, wlen).astype(F32)
    pooled = win / count - a
    y_pool = _dot(pooled.astype(BF16), poolw_ref[...]) * pools_ref[...] * _silu(ag_ref[...])

    zc = _history(cc_ref[...] * cx_ref[...], cch_ref[...] * cxh_ref[...], first)
    cw = convw_ref[...]
    conv = cw[2:3] * zc + cw[1:2] * _shift_rows(zc, 1) + cw[0:1] * _shift_rows(zc, 2)
    y_conv = cb_ref[...] * conv[HALO:] * _silu(cg_ref[...])

    branches = ((y_pool, wmp_ref), (yh_ref[...], wmh_ref), (y_conv, wmc_ref), (yd_ref[...], wmd_ref))
    merged = jnp.zeros((tm, d), F32)
    for bi, (y, w_ref) in enumerate(branches):
        gate = _sigmoid(_dot(h, wmg_ref[:, bi * d:(bi + 1) * d]))
        merged = merged + gate * _dot(y.astype(BF16), w_ref[...])
    out = _dot(merged.astype(BF16), wout_ref[...])
    o_ref[...] = x + mod[:, 2 * d:3 * d] * _rms(out, gpost_ref[...])


def _tail(x2, mod_l, g_pre, g_post, p, y_hgrn, y_diff, w_mg, pool_bd, pool_scale, conv_w,
          wm_pool, wm_hgrn, wm_conv, wm_diff, w_out, *, seq, tm):
    m, d = x2.shape
    tiles_per_batch = seq // tm
    hb = tm // HALO
    cw = POOL_WIDTH
    cur = lambda off: pl.BlockSpec((tm, cw), lambda i, o=off: (i, o // cw))
    halo = lambda off: pl.BlockSpec((HALO, cw), lambda i, o=off: (jnp.maximum(i * hb - 1, 0), o // cw))
    whole = lambda a: pl.BlockSpec(a.shape, lambda i, nd=a.ndim: (0,) * nd)
    weights = (w_mg, pool_bd, pool_scale, conv_w, wm_pool, wm_hgrn, wm_conv, wm_diff, w_out)
    return pl.pallas_call(
        functools.partial(_tail_kernel, d=d, tm=tm, tiles_per_batch=tiles_per_batch),
        out_shape=jax.ShapeDtypeStruct((m, d), F32),
        grid=(m // tm,),
        in_specs=[
            pl.BlockSpec((tm, d), lambda i: (i, 0)),
            pl.BlockSpec((1, 1, 3 * d), lambda i: (i // tiles_per_batch, 0, 0)),
            pl.BlockSpec((1, d), lambda i: (0, 0)),
            pl.BlockSpec((1, d), lambda i: (0, 0)),
            cur(OFF_POOL_IN), cur(OFF_POOL_G), halo(OFF_POOL_IN),
            cur(OFF_CX), cur(OFF_CB), cur(OFF_CC), cur(OFF_CG), halo(OFF_CX), halo(OFF_CC),
            pl.BlockSpec((tm, HGRN_WIDTH), lambda i: (i, 0)),
            pl.BlockSpec((tm, DIFF_WIDTH), lambda i: (i, 0)),
        ] + [whole(a) for a in weights],
        out_specs=pl.BlockSpec((tm, d), lambda i: (i, 0)),
        compiler_params=pltpu.CompilerParams(
            dimension_semantics=("parallel",), vmem_limit_bytes=VMEM_LIMIT),
        name="tail",
    )(x2, mod_l, g_pre, g_post, p, p, p, p, p, p, p, p, p, y_hgrn, y_diff, *weights)


def _block_diag(w):
    g, a, b = w.shape
    out = jnp.zeros((g * a, g * b), w.dtype)
    for i in range(g):
        out = out.at[i * a:(i + 1) * a, i * b:(i + 1) * b].set(w[i])
    return out


def kernel(x, c, w_ada, b_ada, g_pre, g_post, w_in, pool_w, pool_scale, hgrn_lb, hgrn_norm, conv_w, diff_lam, diff_norm, w_merge_pool, w_merge_hgrn, w_merge_conv, w_merge_diff, w_out):
    bsz, seq, d = x.shape
    depth = w_ada.shape[0]
    m = bsz * seq
    tm_proj = min(512, seq)
    tm_tail = min(256, seq)
    tq = min(512, seq)
    tk = min(512, seq)
    tt = min(512, seq)

    mod = _modulation(c, w_ada, b_ada)
    x2 = x.reshape(m, d)
    for l in range(depth):
        mod_l = mod[l].reshape(bsz, 1, 3 * d)
        w_in_l = w_in[l].astype(BF16)
        p, p16 = _inproj(x2, mod_l, g_pre[l].reshape(1, d), w_in_l[:, :MIX_COLS],
                         seq=seq, tm=tm_proj, tn=DIFF_WIDTH)
        y_diff = _diff_attention(p16, diff_lam[l], diff_norm[l], l, bsz=bsz, seq=seq, tq=tq, tk=tk)
        y_hgrn = _hgrn(p, hgrn_lb, hgrn_norm[l], l, bsz=bsz, seq=seq, tt=tt)
        x2 = _tail(x2, mod_l, g_pre[l].reshape(1, d), g_post[l].reshape(1, d), p, y_hgrn, y_diff,
                   w_in_l[:, MIX_COLS:], _block_diag(pool_w[l]).astype(BF16),
                   pool_scale[l].reshape(1, POOL_WIDTH), conv_w[l],
                   w_merge_pool[l].astype(BF16), w_merge_hgrn[l].astype(BF16),
                   w_merge_conv[l].astype(BF16), w_merge_diff[l].astype(BF16),
                   w_out[l].astype(BF16), seq=seq, tm=tm_tail)
    return x2.reshape(bsz, seq, d)
```

```python
import functools
import math

import numpy as np
import jax
import jax.numpy as jnp
from jax import lax
from jax.experimental import pallas as pl
from jax.experimental.pallas import tpu as pltpu

F32 = jnp.float32
BF16 = jnp.bfloat16

POOL_WIDTH = 256
POOL_WINDOWS = (2, 4, 8, 16)
POOL_GROUP_DIM = 64
HGRN_HEADS = 4
HGRN_DK = 64
HGRN_WIDTH = 256
MIN_F = 1e-20
CONV_WIDTH = 256
CONV_K = 3
DIFF_HEADS = 4
DIFF_DH = 64
DIFF_WIDTH = 512
N_BRANCH = 4
EPS = 1e-6
NEG = -1e30
LOG2E = 1.4426950408889634

OFF_POOL_IN, OFF_POOL_G = 0, 256
OFF_HQ, OFF_HF, OFF_HI, OFF_HG = 512, 768, 1024, 1280
OFF_CX, OFF_CB, OFF_CC, OFF_CG = 1536, 1792, 2048, 2304
OFF_DQ, OFF_DK, OFF_DV, OFF_DG = 2560, 3072, 3584, 4096
MIX_COLS = 4608

LANES = 128
SUBLANES = 8
HALO = 16
HGRN_CHUNK = 64
VMEM_LIMIT = 56 * 1024 * 1024


def _sigmoid(x):
    return 1.0 / (1.0 + jnp.exp(-x))


def _gate_sigmoid(x):
    return 0.5 * jnp.tanh(0.5 * x) + 0.5


def _silu(x):
    return x * _gate_sigmoid(x)


def _rms(x, g):
    return x * lax.rsqrt(jnp.mean(x * x, axis=-1, keepdims=True) + EPS) * g


def _dot(a, b):
    return jnp.dot(a, b, preferred_element_type=F32)


def _dot_nt(a, b):
    return lax.dot_general(a, b, (((1,), (1,)), ((), ())), preferred_element_type=F32)


def _split3(x):
    hi = x.astype(BF16)
    r1 = x - hi.astype(F32)
    mid = r1.astype(BF16)
    lo = (r1 - mid.astype(F32)).astype(BF16)
    return hi, mid, lo


def _mod_kernel(c_ref, w_ref, b_ref, o_ref):
    ca = _silu(c_ref[...])
    o_ref[0] = jnp.dot(ca, w_ref[0], preferred_element_type=F32,
                       precision=lax.Precision.HIGHEST) + b_ref[0]


def _modulation(c, w_ada, b_ada):
    depth, d, d3 = w_ada.shape
    bsz = c.shape[0]
    tn = 1024
    return pl.pallas_call(
        _mod_kernel,
        out_shape=jax.ShapeDtypeStruct((depth, bsz, d3), F32),
        grid=(depth, d3 // tn),
        in_specs=[
            pl.BlockSpec((bsz, d), lambda l, j: (0, 0)),
            pl.BlockSpec((1, d, tn), lambda l, j: (l, 0, j)),
            pl.BlockSpec((1, 1, tn), lambda l, j: (l, 0, j)),
        ],
        out_specs=pl.BlockSpec((1, bsz, tn), lambda l, j: (l, 0, j)),
        compiler_params=pltpu.CompilerParams(
            dimension_semantics=("parallel", "parallel"), vmem_limit_bytes=VMEM_LIMIT),
        name="adaln_mod",
    )(c, w_ada, b_ada.reshape(depth, 1, d3))


def _prenorm(x, mod_row, g_pre, d):
    shift = mod_row[:, 0:d]
    scale = mod_row[:, d:2 * d]
    return _rms(x, g_pre) * (1.0 + scale) + shift


def _inproj_kernel(x_ref, mod_ref, g_ref, w_ref, o32_ref, o16_ref, *, d, tn):
    h = _prenorm(x_ref[...], mod_ref[0], g_ref[...], d).astype(BF16)
    for n0 in range(0, MIX_COLS, tn):
        o = _dot(h, w_ref[:, n0:n0 + tn])
        if n0 < OFF_DQ:
            o32_ref[:, n0:n0 + tn] = o
        else:
            if n0 == OFF_DQ:
                o = o * (DIFF_DH ** -0.5 * LOG2E)
            o16_ref[:, n0 - OFF_DQ:n0 - OFF_DQ + tn] = o.astype(BF16)


def _inproj(x2, mod_l, g_pre, w_in16, layer_idx, *, seq, tm, tn):
    m, d = x2.shape
    n = MIX_COLS
    assert tn == DIFF_WIDTH and OFF_DQ % tn == 0 and n % LANES == 0
    tiles_per_batch = seq // tm
    return pl.pallas_call(
        functools.partial(_inproj_kernel, d=d, tn=tn),
        out_shape=(jax.ShapeDtypeStruct((m, OFF_DQ), F32),
                   jax.ShapeDtypeStruct((m, n - OFF_DQ), BF16)),
        grid=(m // tm,),
        in_specs=[
            pl.BlockSpec((tm, d), lambda i: (i, 0)),
            pl.BlockSpec((1, 1, 3 * d), lambda i: (i // tiles_per_batch, 0, 0)),
            pl.BlockSpec((1, d), lambda i: (0, 0)),
            pl.BlockSpec((None, d, n), lambda i: (layer_idx, 0, 0)),
        ],
        out_specs=(pl.BlockSpec((tm, OFF_DQ), lambda i: (i, 0)),
                   pl.BlockSpec((tm, n - OFF_DQ), lambda i: (i, 0))),
        compiler_params=pltpu.CompilerParams(
            dimension_semantics=("parallel",), vmem_limit_bytes=VMEM_LIMIT),
        name="in_proj",
    )(x2, mod_l, g_pre, w_in16)


def _attn_kernel(q_ref, k_ref, v_ref, g_ref, lam_ref, ng_ref, o_ref, s_ref, *, seq, tq, tk, lam_init):
    lane = lax.broadcasted_iota(jnp.int32, (1, LANES), 1)

    def lane_fold(x, op):
        out = x[:, 0:LANES]
        for cb in range(1, tk // LANES):
            out = op(out, x[:, cb * LANES:(cb + 1) * LANES])
        return out

    def tile(i):
        q = q_ref[...]
        zero = jnp.zeros_like(q)
        q_maps = (jnp.where(lane < DIFF_DH, q, zero), jnp.where(lane >= DIFF_DH, q, zero))
        n_blocks = (i + 1) * tq // tk

        mx = [None, None]
        for j in range(n_blocks):
            kj = k_ref[j * tk:(j + 1) * tk, :]
            crosses = (j + 1) * tk - 1 > i * tq
            if crosses:
                row = lax.broadcasted_iota(jnp.int32, (tq, tk), 0) + i * tq
                col = lax.broadcasted_iota(jnp.int32, (tq, tk), 1) + j * tk
                keep = col <= row
            for mp in range(2):
                s = _dot_nt(q_maps[mp], kj)
                if crosses:
                    s = jnp.where(keep, s, NEG)
                s_ref[mp, j] = s
                blk_max = lane_fold(s, jnp.maximum)
                mx[mp] = blk_max if mx[mp] is None else jnp.maximum(mx[mp], blk_max)
        m_row = [jnp.max(v, axis=-1, keepdims=True) for v in mx]

        l_acc = [None, None]
        acc = [None, None]
        for j in range(n_blocks):
            vj = v_ref[j * tk:(j + 1) * tk, :]
            for mp in range(2):
                p = jnp.exp2(s_ref[mp, j] - m_row[mp])
                l_blk = lane_fold(p, jnp.add)
                pv = _dot(p.astype(BF16), vj)
                l_acc[mp] = l_blk if l_acc[mp] is None else l_acc[mp] + l_blk
                acc[mp] = pv if acc[mp] is None else acc[mp] + pv
        l0, l1 = (jnp.sum(v, axis=-1, keepdims=True) for v in l_acc)

        lf = lam_ref[...]
        lam_full = (jnp.exp(jnp.sum(lf[0:1] * lf[1:2], axis=-1, keepdims=True))
                    - jnp.exp(jnp.sum(lf[2:3] * lf[3:4], axis=-1, keepdims=True)) + lam_init)
        o = acc[0] / l0 - lam_full * (acc[1] / l1)
        o = _rms(o, ng_ref[...]) * (1.0 - lam_init)
        o_ref[...] = o * _silu(g_ref[...].astype(F32))

    for i in range(seq // tq):
        pl.when(pl.program_id(2) == i)(functools.partial(tile, i))


def _diff_attention(p16, lam, norm_g, layer_idx, *, bsz, seq, tq, tk):
    m = p16.shape[0]
    nq = seq // tq
    lam_init = 0.8 - 0.6 * math.exp(-0.3 * layer_idx)
    cq, ck, cv, cg = ((off - OFF_DQ) // LANES for off in (OFF_DQ, OFF_DK, OFF_DV, OFF_DG))
    return pl.pallas_call(
        functools.partial(_attn_kernel, seq=seq, tq=tq, tk=tk, lam_init=lam_init),
        out_shape=jax.ShapeDtypeStruct((m, DIFF_WIDTH), F32),
        grid=(bsz, DIFF_HEADS, nq),
        in_specs=[
            pl.BlockSpec((tq, LANES), lambda b, h, i: (b * nq + i, cq + h)),
            pl.BlockSpec((seq, LANES), lambda b, h, i: (b, ck + h)),
            pl.BlockSpec((seq, LANES), lambda b, h, i: (b, cv + h)),
            pl.BlockSpec((tq, LANES), lambda b, h, i: (b * nq + i, cg + h)),
            pl.BlockSpec((4, DIFF_DH), lambda b, h, i: (0, 0)),
            pl.BlockSpec((1, LANES), lambda b, h, i: (0, 0)),
        ],
        out_specs=pl.BlockSpec((tq, LANES), lambda b, h, i: (b * nq + i, h)),
        scratch_shapes=[pltpu.VMEM((2, seq // tk, tq, tk), F32)],
        compiler_params=pltpu.CompilerParams(
            dimension_semantics=("parallel", "parallel", "arbitrary"),
            vmem_limit_bytes=VMEM_LIMIT),
        name="diff_attn",
    )(p16, p16, p16, p16, lam, norm_g.reshape(1, LANES))


def _hgrn_constants():
    c = HGRN_CHUNK
    t = np.arange(c)
    sizes = [c >> k for k in range(int(math.log2(c)))]
    small = [m for m in sizes if m < SUBLANES]
    prefix = [((t[None, :] // m) == (t[:, None] // m)) & (t[None, :] <= t[:, None]) for m in sizes]
    suffix = [((t[None, :] // m) == (t[:, None] // m)) & (t[None, :] > t[:, None]) for m in small]
    masks = [np.eye(c, dtype=bool)]
    m = 1
    while m < c:
        blk = t // m
        masks.append((blk % 2 == 1)[:, None] & (blk[None, :] == (blk - 1)[:, None]))
        m *= 2
    wsum = np.concatenate(prefix + suffix, axis=0).astype(np.float32)
    masks = np.stack([np.tile(mk, (1, HGRN_HEADS)) for mk in masks]).astype(np.float32)
    head = np.arange(HGRN_WIDTH) // HGRN_DK
    bmask = (head[:, None] == head[None, :]).astype(np.float32)
    return wsum, masks, bmask, sizes, small


def _block_end(x, m):
    n, w = x.shape
    x3 = x.reshape(n // m, m, w)
    return jnp.broadcast_to(x3[:, m - 1:m, :], x3.shape).reshape(n, w)


def _hgrn_kernel(q_ref, f_ref, i_ref, g_ref, lb_ref, ng_ref, wsum_ref, masks_ref, bmask_ref,
                 o_ref, st_ref, *, layer_idx, n_chunks, sizes, small):
    c = HGRN_CHUNK

    @pl.when(pl.program_id(1) == 0)
    def _():
        st_ref[...] = jnp.zeros_like(st_ref)

    lb_all = lb_ref[...]
    e = jnp.exp(lb_all - jnp.max(lb_all, axis=0, keepdims=True))
    sm = e / jnp.sum(e, axis=0, keepdims=True)
    lb = sm[0:1]
    for r in range(1, layer_idx + 1):
        lb = lb + sm[r:r + 1]
    lb = lb - sm[0:1]

    bmask = bmask_ref[...]
    bmask16 = bmask.astype(BF16)
    wsum = wsum_ref[...]
    chunks = [slice(ci * c, (ci + 1) * c) for ci in range(n_chunks)]

    z = f_ref[...]
    fc = jnp.maximum(lb + (1.0 - lb) * _sigmoid(z), MIN_F)
    g = jnp.log(fc)
    kk = (1.0 - lb) * _sigmoid(-z)
    qf = _silu(q_ref[...])
    v16 = i_ref[...].astype(BF16)

    g_hi = g.astype(BF16)
    g_lo = (g - g_hi.astype(F32)).astype(BF16)
    d = [_dot(wsum, g_hi[rows]) + _dot(wsum, g_lo[rows]) for rows in chunks]
    nsz = len(sizes)
    pre = {m: jnp.concatenate([dc[k * c:(k + 1) * c] for dc in d], axis=0) for k, m in enumerate(sizes)}
    suf = {m: jnp.concatenate([dc[(nsz + k) * c:(nsz + k + 1) * c] for dc in d], axis=0)
           for k, m in enumerate(small)}
    for m in sizes:
        if m not in suf:
            suf[m] = _block_end(pre[m], m) - pre[m]

    sides = [(qf, kk), (qf * fc, kk)]
    for m in reversed(sizes[1:]):
        sides.append((qf * jnp.exp(pre[m]), kk * jnp.exp(suf[m])))
    sides = [(qs.astype(BF16), ks.astype(BF16)) for qs, ks in sides]
    b = pre[c]
    eb_last = jnp.exp(b)
    bq = (qf * eb_last).astype(BF16)
    kd = (kk * jnp.exp(suf[c])).astype(BF16)

    def block_diag(x):
        return jnp.concatenate([x] * HGRN_HEADS, axis=0) * bmask16

    o_intra = []
    for rows in chunks:
        a_cat = None
        for lv, (qs, ks) in enumerate(sides):
            term = masks_ref[lv] * _dot_nt(qs[rows], block_diag(ks[rows]))
            a_cat = term if a_cat is None else a_cat + term
        o_intra.append(_dot(a_cat.astype(BF16), block_diag(v16[rows])))

    st = st_ref[...]
    outs = []
    for ci, rows in enumerate(chunks):
        outs.append(o_intra[ci] + _dot_nt(bq[rows], st.astype(BF16)))
        last = (ci + 1) * c - 1
        st = st * eb_last[last:last + 1] + bmask * _dot(v16[rows].T, kd[rows])
    st_ref[...] = st

    o = jnp.concatenate(outs, axis=0)
    ms = _dot(o * o, bmask * (1.0 / HGRN_DK))
    y = o * lax.rsqrt(ms + EPS) * ng_ref[...]
    o_ref[...] = y * _silu(g_ref[...])


def _hgrn(p, hgrn_lb, norm_g, layer_idx, *, bsz, seq, tt):
    m = p.shape[0]
    nt = seq // tt
    wsum, masks, bmask, sizes, small = _hgrn_constants()
    w = HGRN_WIDTH
    col = lambda off: off // w
    spec = lambda off: pl.BlockSpec((tt, w), lambda b, t, o=off: (b * nt + t, col(o)))
    whole = lambda a: pl.BlockSpec(a.shape, lambda b, t, nd=a.ndim: (0,) * nd)
    ng = jnp.tile(norm_g.reshape(1, HGRN_DK), (1, HGRN_HEADS))
    consts = (hgrn_lb, ng, jnp.asarray(wsum, BF16), jnp.asarray(masks), jnp.asarray(bmask))
    return pl.pallas_call(
        functools.partial(_hgrn_kernel, layer_idx=layer_idx, n_chunks=tt // HGRN_CHUNK,
                          sizes=tuple(sizes), small=tuple(small)),
        out_shape=jax.ShapeDtypeStruct((m, w), F32),
        grid=(bsz, nt),
        in_specs=[spec(OFF_HQ), spec(OFF_HF), spec(OFF_HI), spec(OFF_HG)] + [whole(a) for a in consts],
        out_specs=pl.BlockSpec((tt, w), lambda b, t: (b * nt + t, 0)),
        scratch_shapes=[pltpu.VMEM((w, w), F32)],
        compiler_params=pltpu.CompilerParams(
            dimension_semantics=("parallel", "arbitrary"), vmem_limit_bytes=VMEM_LIMIT),
        name="hgrn2",
    )(p, p, p, p, *consts)


def _history(cur, halo, first):
    return jnp.concatenate([jnp.where(first, 0.0, halo), cur], axis=0)


def _shift_rows(x, k):
    return pltpu.roll(x, k, 0)


def _tail_kernel(x_ref, mod_ref, gpre_ref, gpost_ref,
                 ain_ref, ag_ref, ainh_ref, cx_ref, cb_ref, cc_ref, cg_ref, cxh_ref, cch_ref,
                 yh_ref, yd_ref,
                 poolw_ref, pools_ref, convw_ref,
                 wmp_ref, wmh_ref, wmc_ref, wmd_ref, wout_ref,
                 *gate_w_and_out, d, tm, tiles_per_batch, col_block):
    wmg_refs, o_ref = gate_w_and_out[:-1], gate_w_and_out[-1]
    t_idx = pl.program_id(0) % tiles_per_batch
    first = t_idx == 0
    x = x_ref[...]
    mod = mod_ref[0]
    h = _prenorm(x, mod, gpre_ref[...], d).astype(BF16)

    a = ain_ref[...]
    s = _history(a, ainh_ref[...], first)
    lane = lax.broadcasted_iota(jnp.int32, (1, POOL_WIDTH), 1)
    grp = lane // POOL_GROUP_DIM
    win = None
    for gi, w in enumerate(POOL_WINDOWS):
        s = s + _shift_rows(s, w // 2)
        win = s if win is None else jnp.where(grp >= gi, s, win)
    win = win[HALO:]
    wlen = jnp.where(grp == 0, POOL_WINDOWS[0], jnp.where(grp == 1, POOL_WINDOWS[1],
                     jnp.where(grp == 2, POOL_WINDOWS[2], POOL_WINDOWS[3])))
    pos = t_idx * tm + lax.broadcasted_iota(jnp.int32, (tm, 1), 0)
    count = jnp.minimum(pos + 1, wlen).astype(F32)
    pooled = win / count - a
    y_pool = _dot(pooled.astype(BF16), poolw_ref[...]) * pools_ref[...] * _silu(ag_ref[...])

    zc = _history(cc_ref[...] * cx_ref[...], cch_ref[...] * cxh_ref[...], first)
    cw = convw_ref[...]
    conv = cw[2:3] * zc + cw[1:2] * _shift_rows(zc, 1) + cw[0:1] * _shift_rows(zc, 2)
    y_conv = cb_ref[...] * conv[HALO:] * _silu(cg_ref[...])

    branches = ((y_pool.astype(BF16), wmp_ref), (yh_ref[...].astype(BF16), wmh_ref),
                (y_conv.astype(BF16), wmc_ref), (yd_ref[...].astype(BF16), wmd_ref))
    out = None
    for c0 in range(0, d, col_block):
        cols = slice(c0, c0 + col_block)
        merged = None
        for bi, (y, w_ref) in enumerate(branches):
            gate = _gate_sigmoid(_dot(h, wmg_refs[(bi * d + c0) // col_block][...]))
            term = gate * _dot(y, w_ref[:, cols])
            merged = term if merged is None else merged + term
        part = _dot(merged.astype(BF16), wout_ref[cols, :])
        out = part if out is None else out + part
    o_ref[...] = x + mod[:, 2 * d:3 * d] * _rms(out, gpost_ref[...])


def _tail(x2, mod_l, g_pre, g_post, p, y_hgrn, y_diff, w_in16, layer_idx, pool_bd, pool_scale, conv_w,
          wm_pool, wm_hgrn, wm_conv, wm_diff, w_out, *, seq, tm):
    m, d = x2.shape
    tiles_per_batch = seq // tm
    hb = tm // HALO
    cw = POOL_WIDTH
    col_block = 512
    assert MIX_COLS % col_block == 0 and d % col_block == 0
    cur = lambda off: pl.BlockSpec((tm, cw), lambda i, o=off: (i, o // cw))
    halo = lambda off: pl.BlockSpec((HALO, cw), lambda i, o=off: (jnp.maximum(i * hb - 1, 0), o // cw))
    whole = lambda a: pl.BlockSpec(a.shape, lambda i, nd=a.ndim: (0,) * nd)
    weights = (pool_bd, pool_scale, conv_w, wm_pool, wm_hgrn, wm_conv, wm_diff, w_out)
    n_gate_blocks = N_BRANCH * d // col_block
    gate_specs = [pl.BlockSpec((None, d, col_block),
                               lambda i, k=k: (layer_idx, 0, MIX_COLS // col_block + k))
                  for k in range(n_gate_blocks)]
    return pl.pallas_call(
        functools.partial(_tail_kernel, d=d, tm=tm, tiles_per_batch=tiles_per_batch,
                          col_block=col_block),
        out_shape=jax.ShapeDtypeStruct((m, d), F32),
        grid=(m // tm,),
        in_specs=[
            pl.BlockSpec((tm, d), lambda i: (i, 0)),
            pl.BlockSpec((1, 1, 3 * d), lambda i: (i // tiles_per_batch, 0, 0)),
            pl.BlockSpec((1, d), lambda i: (0, 0)),
            pl.BlockSpec((1, d), lambda i: (0, 0)),
            cur(OFF_POOL_IN), cur(OFF_POOL_G), halo(OFF_POOL_IN),
            cur(OFF_CX), cur(OFF_CB), cur(OFF_CC), cur(OFF_CG), halo(OFF_CX), halo(OFF_CC),
            pl.BlockSpec((tm, HGRN_WIDTH), lambda i: (i, 0)),
            pl.BlockSpec((tm, DIFF_WIDTH), lambda i: (i, 0)),
        ] + [whole(a) for a in weights] + gate_specs,
        out_specs=pl.BlockSpec((tm, d), lambda i: (i, 0)),
        compiler_params=pltpu.CompilerParams(
            dimension_semantics=("parallel",), vmem_limit_bytes=VMEM_LIMIT),
        name="tail",
    )(x2, mod_l, g_pre, g_post, p, p, p, p, p, p, p, p, p, y_hgrn, y_diff, *weights,
      *([w_in16] * n_gate_blocks))


def _block_diag(w):
    g, a, b = w.shape
    out = jnp.zeros((g * a, g * b), w.dtype)
    for i in range(g):
        out = out.at[i * a:(i + 1) * a, i * b:(i + 1) * b].set(w[i])
    return out


def kernel(x, c, w_ada, b_ada, g_pre, g_post, w_in, pool_w, pool_scale, hgrn_lb, hgrn_norm, conv_w, diff_lam, diff_norm, w_merge_pool, w_merge_hgrn, w_merge_conv, w_merge_diff, w_out):
    bsz, seq, d = x.shape
    depth = w_ada.shape[0]
    m = bsz * seq
    tm_proj = min(512, seq)
    tm_tail = min(512, seq)
    tq = min(512, seq)
    tk = min(512, seq)
    tt = min(512, seq)

    mod = _modulation(c, w_ada, b_ada)
    x2 = x.reshape(m, d)
    w_in16 = w_in.astype(BF16)
    for l in range(depth):
        mod_l = mod[l].reshape(bsz, 1, 3 * d)
        p, p16 = _inproj(x2, mod_l, g_pre[l].reshape(1, d), w_in16, l,
                         seq=seq, tm=tm_proj, tn=DIFF_WIDTH)
        y_diff = _diff_attention(p16, diff_lam[l], diff_norm[l], l, bsz=bsz, seq=seq, tq=tq, tk=tk)
        y_hgrn = _hgrn(p, hgrn_lb, hgrn_norm[l], l, bsz=bsz, seq=seq, tt=tt)
        x2 = _tail(x2, mod_l, g_pre[l].reshape(1, d), g_post[l].reshape(1, d), p, y_hgrn, y_diff,
                   w_in16, l, _block_diag(pool_w[l]).astype(BF16),
                   pool_scale[l].reshape(1, POOL_WIDTH), conv_w[l],
                   w_merge_pool[l].astype(BF16), w_merge_hgrn[l].astype(BF16),
                   w_merge_conv[l].astype(BF16), w_merge_diff[l].astype(BF16),
                   w_out[l].astype(BF16), seq=seq, tm=tm_tail)
    return x2.reshape(bsz, seq, d)
```
